```python
import math
import jax, jax.numpy as jnp
from jax import lax
import numpy as np

D_MODEL = 1024
BATCH = 32
SEQ = 2048
DEPTH = 1
DEC_BATCH = 8
DEC_SEQ = 32
PAST_LEN = 4096

CHUNK = 64
Q_BLOCK = 128
D_MIX = D_MODEL
RET_HEADS = 4
RET_DK = D_MIX // 2 // RET_HEADS
RET_DV = RET_DK
RET_WIDTH = RET_HEADS * RET_DV
DIFF_HEADS = 4
DIFF_DV = (D_MIX - RET_WIDTH) // DIFF_HEADS
DIFF_DK = DIFF_DV // 2
DIFF_WIDTH = DIFF_HEADS * DIFF_DV
IN_WIDTHS = (RET_HEADS * RET_DK, RET_HEADS * RET_DK, RET_WIDTH, RET_WIDTH,
             2 * DIFF_HEADS * DIFF_DK, 2 * DIFF_HEADS * DIFF_DK, DIFF_WIDTH)
D_IN = sum(IN_WIDTHS)
D_FF = 4 * D_MODEL
ROPE_THETA = 10000.0
NORM_EPS = 1e-6
SUBLN_EPS = 1e-5
NEG_INF = -1e30

kernel_name = 'hybrid_retention_diffattn_stream_step'


def rms_normalize(x, eps):
    xf = x.astype(jnp.float32)
    return xf * lax.rsqrt(jnp.mean(xf * xf, axis=-1, keepdims=True) + eps)


def rmsnorm(x, w, eps=NORM_EPS):
    return (rms_normalize(x, eps) * w.astype(jnp.float32)).astype(x.dtype)


def rope(x, pos):
    d = x.shape[-1]
    inv = ROPE_THETA ** (-jnp.arange(0, d, 2, dtype=jnp.float32) / d)
    ang = pos.astype(jnp.float32)[:, None] * inv[None, :]
    cos = jnp.cos(ang)[None, :, None, :]
    sin = jnp.sin(ang)[None, :, None, :]
    xf = x.astype(jnp.float32)
    x1, x2 = xf[..., : d // 2], xf[..., d // 2:]
    return jnp.concatenate([x1 * cos - x2 * sin, x2 * cos + x1 * sin], axis=-1).astype(x.dtype)


def retention_log_gamma():
    return jnp.log1p(-(2.0 ** (-5.0 - jnp.arange(RET_HEADS, dtype=jnp.float32))))[:, None]


def retention_block(state, q, k, v):
    L = q.shape[2]
    lg = retention_log_gamma()
    idx = jnp.arange(L, dtype=jnp.float32)
    rel = idx[:, None] - idx[None, :]
    decay = jnp.where(rel >= 0, jnp.exp(lg[:, :, None] * jnp.maximum(rel, 0.0)), 0.0)
    scores = jnp.einsum('bhld,bhmd->bhlm', q, k) * decay
    o = jnp.einsum('bhlm,bhmv->bhlv', scores, v)
    o = o + jnp.einsum('bhld,bhdv->bhlv', q * jnp.exp(lg * (idx + 1.0))[:, :, None], state)
    new_state = jnp.exp(lg * L)[..., None] * state + jnp.einsum(
        'bhld,bhlv->bhdv', k * jnp.exp(lg * (L - 1.0 - idx))[:, :, None], v)
    return new_state, o


def retention_scan(state, q, k, v, block_len):
    B, L, H, _ = q.shape
    nc = L // block_len

    def to_blocks(t):
        return t.astype(jnp.float32).reshape(B, nc, block_len, H, t.shape[-1]).transpose(1, 0, 3, 2, 4)

    state, o = lax.scan(lambda s, xs: retention_block(s, *xs), state,
                        (to_blocks(q), to_blocks(k), to_blocks(v)))
    o = o.transpose(1, 0, 3, 2, 4).reshape(B, L, H, v.shape[-1])
    return o, state


def diff_attention_core(q, k, v, mask, lam, subln_w, lambda_init):
    B, Lq = q.shape[:2]
    Lk = k.shape[1]
    s = jnp.einsum('bqhd,bkhd->bhqk', q, k).astype(jnp.float32) * (DIFF_DK ** -0.5)
    s = jnp.where(mask[None, None], s, NEG_INF)
    p = jax.nn.softmax(s, axis=-1).reshape(B, DIFF_HEADS, 2, Lq, Lk)
    a = p[:, :, 0] - lam * p[:, :, 1]
    o = jnp.einsum('bhqk,bkhv->bqhv', a.astype(v.dtype), v)
    o = rms_normalize(o, SUBLN_EPS) * subln_w.astype(jnp.float32) * (1.0 - lambda_init)
    return o.astype(v.dtype)


def hybrid_layer(x, pos, ret_state, k_past, v_past, norm_mix_pre, norm_mix_post, w_in, diff_subln,
                 lambda_q1, lambda_k1, lambda_q2, lambda_k2, w_out, norm_ffn_pre, norm_ffn_post,
                 w_up, w_down, lambda_init):
    B, L, _ = x.shape
    h = rmsnorm(x, norm_mix_pre)
    z = h @ w_in
    cuts = np.cumsum(IN_WIDTHS)[:-1].tolist()
    rq, rk, rv, rg, dq, dk, dv = jnp.split(z, cuts, axis=-1)

    rq = rope(rq.reshape(B, L, RET_HEADS, RET_DK), pos)
    rk = rope(rk.reshape(B, L, RET_HEADS, RET_DK), pos) * (RET_DK ** -0.5)
    rv = rv.reshape(B, L, RET_HEADS, RET_DV)
    block_len = CHUNK if k_past is None else L
    ro, new_ret_state = retention_scan(ret_state.astype(jnp.float32), rq, rk, rv, block_len)
    ro = rms_normalize(ro, NORM_EPS).reshape(B, L, RET_WIDTH)
    ret_out = (jax.nn.silu(rg.astype(jnp.float32)) * ro).astype(x.dtype)

    dq = rope(dq.reshape(B, L, 2 * DIFF_HEADS, DIFF_DK), pos)
    dk = rope(dk.reshape(B, L, 2 * DIFF_HEADS, DIFF_DK), pos)
    dv = dv.reshape(B, L, DIFF_HEADS, DIFF_DV)
    lam = (jnp.exp(jnp.sum(lambda_q1.astype(jnp.float32) * lambda_k1.astype(jnp.float32)))
           - jnp.exp(jnp.sum(lambda_q2.astype(jnp.float32) * lambda_k2.astype(jnp.float32)))
           + lambda_init)
    if k_past is None:
        kpos = jnp.arange(L)

        def query_block(i):
            qb = lax.dynamic_slice_in_dim(dq, i * Q_BLOCK, Q_BLOCK, axis=1)
            qpos = i * Q_BLOCK + jnp.arange(Q_BLOCK)
            mask = kpos[None, :] < (qpos[:, None] // CHUNK + 1) * CHUNK
            return diff_attention_core(qb, dk, dv, mask, lam, diff_subln, lambda_init)

        do = lax.map(query_block, jnp.arange(L // Q_BLOCK))
        do = do.transpose(1, 0, 2, 3, 4).reshape(B, L, DIFF_HEADS, DIFF_DV)
    else:
        k_all = jnp.concatenate([k_past.astype(dk.dtype), dk], axis=1)
        v_all = jnp.concatenate([v_past.astype(dv.dtype), dv], axis=1)
        mask = jnp.ones((L, k_all.shape[1]), dtype=bool)
        do = diff_attention_core(dq, k_all, v_all, mask, lam, diff_subln, lambda_init)
    diff_out = do.reshape(B, L, DIFF_WIDTH)

    mix = jnp.concatenate([ret_out, diff_out], axis=-1) @ w_out
    x = x + rmsnorm(mix, norm_mix_post)

    u = rmsnorm(x, norm_ffn_pre) @ w_up
    f = jnp.square(jax.nn.relu(u)) @ w_down
    x = x + rmsnorm(f, norm_ffn_post)
    return x, dk, dv, new_ret_state


def setup_inputs(seed: int = 0) -> dict:
    key = jax.random.key(seed)
    ks = jax.random.split(key, 20)
    f32 = jnp.float32

    def gain(k, n):
        return 1.0 + 0.02 * jax.random.normal(k, (DEPTH, n), f32)

    return {
        'x_prompt': jax.random.normal(ks[0], (BATCH, SEQ, D_MODEL), f32),
        'x_sample': jax.random.normal(ks[1], (DEC_BATCH, DEC_SEQ, D_MODEL), f32),
        'cache_diff_k': jax.random.normal(ks[2], (DEPTH, DEC_BATCH, PAST_LEN, 2 * DIFF_HEADS, DIFF_DK), f32),
        'cache_diff_v': jax.random.normal(ks[3], (DEPTH, DEC_BATCH, PAST_LEN, DIFF_HEADS, DIFF_DV), f32),
        'state_ret': 0.5 * jax.random.normal(ks[4], (DEPTH, DEC_BATCH, RET_HEADS, RET_DK, RET_DV), f32),
        'norm_mix_pre': gain(ks[5], D_MODEL),
        'norm_mix_post': gain(ks[6], D_MODEL),
        'w_in': jax.random.normal(ks[7], (DEPTH, D_MODEL, D_IN), f32) * D_MODEL ** -0.5,
        'diff_subln': gain(ks[8], DIFF_DV),
        'lambda_q1': 0.1 * jax.random.normal(ks[9], (DEPTH, DIFF_DK), f32),
        'lambda_k1': 0.1 * jax.random.normal(ks[10], (DEPTH, DIFF_DK), f32),
        'lambda_q2': 0.1 * jax.random.normal(ks[11], (DEPTH, DIFF_DK), f32),
        'lambda_k2': 0.1 * jax.random.normal(ks[12], (DEPTH, DIFF_DK), f32),
        'w_out': jax.random.normal(ks[13], (DEPTH, D_MIX, D_MODEL), f32) * D_MIX ** -0.5,
        'norm_ffn_pre': gain(ks[14], D_MODEL),
        'norm_ffn_post': gain(ks[15], D_MODEL),
        'w_up': jax.random.normal(ks[16], (DEPTH, D_MODEL, D_FF), f32) * D_MODEL ** -0.5,
        'w_down': jax.random.normal(ks[17], (DEPTH, D_FF, D_MODEL), f32) * D_FF ** -0.5,
    }


def reference(x_prompt, x_sample, cache_diff_k, cache_diff_v, state_ret, norm_mix_pre, norm_mix_post,
              w_in, diff_subln, lambda_q1, lambda_k1, lambda_q2, lambda_k2, w_out, norm_ffn_pre,
              norm_ffn_post, w_up, w_down):
    Bp, Lp, _ = x_prompt.shape
    Ls = x_sample.shape[1]
    past = cache_diff_k.shape[2]
    pos_prompt = jnp.arange(Lp)
    pos_sample = past + jnp.arange(Ls)
    yp, ys = x_prompt, x_sample
    kp_l, vp_l, sp_l, ks_l, vs_l, ss_l = [], [], [], [], [], []
    for l in range(DEPTH):
        lambda_init = 0.8 - 0.6 * math.exp(-0.3 * l)
        weights = (norm_mix_pre[l], norm_mix_post[l], w_in[l], diff_subln[l], lambda_q1[l], lambda_k1[l],
                   lambda_q2[l], lambda_k2[l], w_out[l], norm_ffn_pre[l], norm_ffn_post[l], w_up[l], w_down[l])
        zero_state = jnp.zeros((Bp, RET_HEADS, RET_DK, RET_DV), jnp.float32)
        yp, kp, vp, sp = hybrid_layer(yp, pos_prompt, zero_state, None, None, *weights, lambda_init)
        ys, kss, vss, sss = hybrid_layer(ys, pos_sample, state_ret[l], cache_diff_k[l], cache_diff_v[l],
                                         *weights, lambda_init)
        kp_l.append(kp); vp_l.append(vp); sp_l.append(sp.astype(x_prompt.dtype))
        ks_l.append(kss); vs_l.append(vss); ss_l.append(sss.astype(state_ret.dtype))
    k_prompt = jnp.stack(kp_l)
    v_prompt = jnp.stack(vp_l)
    ret_state_prompt = jnp.stack(sp_l)
    k_sample = jnp.stack(ks_l)
    v_sample = jnp.stack(vs_l)
    ret_state_sample = jnp.stack(ss_l)
    return (yp, ys, k_prompt, v_prompt, ret_state_prompt, k_sample, v_sample, ret_state_sample)
```

```python
import functools
import math

import jax
import jax.numpy as jnp
from jax import lax
from jax.experimental import pallas as pl
from jax.experimental.pallas import tpu as pltpu

D_MODEL = 1024
CHUNK = 64
HEAD = 128
RET_HEADS = 4
DIFF_HEADS = 4
DIFF_DK = 64
SEG = 512
N_SEG = 7
D_IN = N_SEG * SEG
D_FF = 4 * D_MODEL
ROPE_THETA = 10000.0
NORM_EPS = 1e-6
SUBLN_EPS = 1e-5
NEG_INF = -1e30

VMEM_LIMIT_BYTES = 56 * 1024 * 1024

F32 = jnp.float32
BF16 = jnp.bfloat16

_NT = (((1,), (1,)), ((), ()))
_TN = (((0,), (0,)), ((), ()))


def _params(*sem):
    return pltpu.CompilerParams(dimension_semantics=sem, vmem_limit_bytes=VMEM_LIMIT_BYTES)


def _const_spec(shape):
    return pl.BlockSpec(shape, lambda *_: (0,) * len(shape), pipeline_mode=pl.Buffered(1))


def _rope_table_kernel(c128_ref, s128_ref, c64_ref, s64_ref, *, seq, pos0):
    n = c128_ref.shape[0]
    lane = lax.broadcasted_iota(jnp.int32, (n, HEAD), 1)
    pos = (lax.broadcasted_iota(jnp.int32, (n, HEAD), 0) % seq + pos0).astype(F32)
    log_theta = math.log(ROPE_THETA)

    def table(d):
        half = d // 2
        r = lane % d
        i = (r % half).astype(F32)
        inv = jnp.exp(i * (-2.0 * log_theta / d))
        ang = pos * inv
        sign = jnp.where(r < half, -1.0, 1.0).astype(F32)
        return jnp.cos(ang), jnp.sin(ang) * sign

    c128_ref[...], s128_ref[...] = table(HEAD)
    c64_ref[...], s64_ref[...] = table(DIFF_DK)


def _rope_tables(n_rows, seq, pos0):
    shp = jax.ShapeDtypeStruct((n_rows, HEAD), F32)
    return pl.pallas_call(
        functools.partial(_rope_table_kernel, seq=seq, pos0=pos0),
        out_shape=(shp,) * 4,
        name="rope_tables",
    )()


def _proj_kernel(x_ref, nw_ref, w_ref, c128_ref, s128_ref, c64_ref, s64_ref,
                 rq_ref, rk_ref, rv_ref, rg_ref, dq_ref, dk_ref, dv_ref, dkb_ref, dvb_ref):
    x = x_ref[...]
    ms = jnp.mean(x * x, axis=-1, keepdims=True)
    h = (x * lax.rsqrt(ms + NORM_EPS) * nw_ref[...]).astype(BF16)

    def seg(s):
        return jnp.dot(h, w_ref[:, s * SEG:(s + 1) * SEG], preferred_element_type=F32)

    c128, s128 = c128_ref[...], s128_ref[...]
    c64, s64 = c64_ref[...], s64_ref[...]
    tm = x.shape[0]
    lane = lax.broadcasted_iota(jnp.int32, (tm, HEAD), 1)
    first_half64 = (lane % DIFF_DK) < (DIFF_DK // 2)

    def rope128(z):
        return z * c128 + pltpu.roll(z, HEAD // 2, 1) * s128

    def rope64(z):
        rot = jnp.where(first_half64, pltpu.roll(z, HEAD - DIFF_DK // 2, 1),
                        pltpu.roll(z, DIFF_DK // 2, 1))
        return z * c64 + rot * s64

    z = seg(0)
    for hd in range(SEG // HEAD):
        sl = slice(hd * HEAD, (hd + 1) * HEAD)
        rq_ref[:, sl] = rope128(z[:, sl]).astype(BF16)
    z = seg(1)
    for hd in range(SEG // HEAD):
        sl = slice(hd * HEAD, (hd + 1) * HEAD)
        rk_ref[:, sl] = (rope128(z[:, sl]) * (HEAD ** -0.5)).astype(BF16)
    rv_ref[...] = seg(2).astype(BF16)
    rg_ref[...] = seg(3).astype(BF16)
    z = seg(4)
    for hd in range(SEG // HEAD):
        sl = slice(hd * HEAD, (hd + 1) * HEAD)
        dq_ref[:, sl] = (rope64(z[:, sl]) * (DIFF_DK ** -0.5)).astype(BF16)
    z = seg(5)
    for hd in range(SEG // HEAD):
        sl = slice(hd * HEAD, (hd + 1) * HEAD)
        kr = rope64(z[:, sl])
        dk_ref[:, sl] = kr
        dkb_ref[:, sl] = kr.astype(BF16)
    z = seg(6)
    dv_ref[...] = z
    dvb_ref[...] = z.astype(BF16)


def _project(x, norm_w, w_in, tables, tm):
    t = x.shape[0]
    nt = tables[0].shape[0] // tm
    row = pl.BlockSpec((tm, D_MODEL), lambda i: (i, 0))
    out = pl.BlockSpec((tm, SEG), lambda i: (i, 0))
    tab = pl.BlockSpec((tm, HEAD), lambda i: (i % nt, 0))
    b16 = jax.ShapeDtypeStruct((t, SEG), BF16)
    f32 = jax.ShapeDtypeStruct((t, SEG), F32)
    return pl.pallas_call(
        _proj_kernel,
        grid=(t // tm,),
        in_specs=[row, _const_spec((1, D_MODEL)), _const_spec((D_MODEL, D_IN)), tab, tab, tab, tab],
        out_specs=[out] * 9,
        out_shape=[b16, b16, b16, b16, b16, f32, f32, b16, b16],
        compiler_params=_params("parallel"),
        name="in_proj",
    )(x, norm_w, w_in, *tables)


def _log_gamma(h):
    return math.log1p(-(2.0 ** (-5.0 - h)))


def _ret_kernel(*refs, blk, has_init):
    if has_init:
        q_ref, k_ref, v_ref, g_ref, s0_ref, o_ref, sout_ref, state, dmat = refs
    else:
        q_ref, k_ref, v_ref, g_ref, o_ref, sout_ref, state, dmat = refs
    b, c = pl.program_id(0), pl.program_id(1)

    @pl.when((b == 0) & (c == 0))
    def _():
        rel = (lax.broadcasted_iota(jnp.int32, (blk, blk), 0)
               - lax.broadcasted_iota(jnp.int32, (blk, blk), 1))
        relf = jnp.maximum(rel, 0).astype(F32)
        for h in range(RET_HEADS):
            dmat[h] = jnp.where(rel >= 0, jnp.exp(_log_gamma(h) * relf), 0.0)

    @pl.when(c == 0)
    def _():
        if has_init:
            state[...] = s0_ref[0]
        else:
            state[...] = jnp.zeros_like(state)

    idx = lax.broadcasted_iota(jnp.int32, (blk, 1), 0).astype(F32)
    for h in range(RET_HEADS):
        sl = slice(h * HEAD, (h + 1) * HEAD)
        lg = _log_gamma(h)
        q, k, v = q_ref[:, sl], k_ref[:, sl], v_ref[:, sl]
        st = state[h]
        s = lax.dot_general(q, k, _NT, preferred_element_type=F32) * dmat[h]
        o = jnp.dot(s.astype(BF16), v, preferred_element_type=F32)
        qd = (q.astype(F32) * jnp.exp(lg * (idx + 1.0))).astype(BF16)
        o = o + jnp.dot(qd, st.astype(BF16), preferred_element_type=F32)
        kd = (k.astype(F32) * jnp.exp(lg * (blk - 1.0 - idx))).astype(BF16)
        state[h] = math.exp(lg * blk) * st + lax.dot_general(kd, v, _TN, preferred_element_type=F32)
        ro = o * lax.rsqrt(jnp.mean(o * o, axis=-1, keepdims=True) + NORM_EPS)
        g = g_ref[:, sl].astype(F32)
        o_ref[:, sl] = (g * (1.0 / (1.0 + jnp.exp(-g))) * ro).astype(BF16)

    @pl.when(c == pl.num_programs(1) - 1)
    def _():
        sout_ref[0] = state[...]


def _retention(rq, rk, rv, rg, state0, batch, seq, blk):
    nc = seq // blk
    tok = pl.BlockSpec((blk, SEG), lambda b, c: (b * nc + c, 0))
    st_spec = pl.BlockSpec((1, RET_HEADS, HEAD, HEAD), lambda b, c: (b, 0, 0, 0))
    has_init = state0 is not None
    args = (rq, rk, rv, rg) + ((state0,) if has_init else ())
    return pl.pallas_call(
        functools.partial(_ret_kernel, blk=blk, has_init=has_init),
        grid=(batch, nc),
        in_specs=[tok] * 4 + ([st_spec] if has_init else []),
        out_specs=[tok, st_spec],
        out_shape=[jax.ShapeDtypeStruct((batch * seq, SEG), BF16),
                   jax.ShapeDtypeStruct((batch, RET_HEADS, HEAD, HEAD), F32)],
        scratch_shapes=[pltpu.VMEM((RET_HEADS, HEAD, HEAD), F32),
                        pltpu.VMEM((RET_HEADS, blk, blk), F32)],
        compiler_params=_params("arbitrary", "arbitrary"),
        name="retention",
    )(*args)


def _lambda(lq1_ref, lk1_ref, lq2_ref, lk2_ref, lambda_init):
    a = jnp.sum(lq1_ref[...] * lk1_ref[...], axis=-1, keepdims=True)
    b = jnp.sum(lq2_ref[...] * lk2_ref[...], axis=-1, keepdims=True)
    return jnp.exp(a) - jnp.exp(b) + lambda_init


def _stack_heads(q):
    lane = lax.broadcasted_iota(jnp.int32, q.shape, 1)
    zero = jnp.zeros_like(q)
    return jnp.concatenate([jnp.where(lane < DIFF_DK, q, zero),
                            jnp.where(lane >= DIFF_DK, q, zero)], axis=0)


def _diff_finish(acc, l, n, lam, sub, lambda_init):
    o = acc[:n] * (1.0 / l[:n]) - lam * (acc[n:] * (1.0 / l[n:]))
    o = o * lax.rsqrt(jnp.mean(o * o, axis=-1, keepdims=True) + SUBLN_EPS)
    return o * sub * (1.0 - lambda_init)


def _attn_kernel(lq1_ref, lk1_ref, lq2_ref, lk2_ref, sub_ref, q_ref, k_ref, v_ref, o_ref,
                 m_ref, l_ref, acc_ref, *, bq, lambda_init):
    qi = pl.program_id(2)
    qs = _stack_heads(q_ref[...])
    m_ref[...] = jnp.full_like(m_ref, NEG_INF)
    l_ref[...] = jnp.zeros_like(l_ref)
    acc_ref[...] = jnp.zeros_like(acc_ref)

    def step(j, masked):
        start = pl.multiple_of(j * bq, bq)
        k = k_ref[pl.ds(start, bq), :]
        v = v_ref[pl.ds(start, bq), :]
        s = lax.dot_general(qs, k, _NT, preferred_element_type=F32)
        if masked:
            r = lax.broadcasted_iota(jnp.int32, s.shape, 0) % bq
            c = lax.broadcasted_iota(jnp.int32, s.shape, 1)
            s = jnp.where(c // CHUNK <= r // CHUNK, s, NEG_INF)
        m_prev = m_ref[...]
        m_new = jnp.maximum(m_prev, jnp.max(s, axis=-1, keepdims=True))
        alpha = jnp.exp(m_prev - m_new)
        p = jnp.exp(s - m_new)
        l_ref[...] = alpha * l_ref[...] + jnp.sum(p, axis=-1, keepdims=True)
        acc_ref[...] = alpha * acc_ref[...] + jnp.dot(p.astype(BF16), v, preferred_element_type=F32)
        m_ref[...] = m_new

    def body(j, carry):
        step(j, False)
        return carry

    lax.fori_loop(0, qi, body, 0)
    step(qi, True)

    lam = _lambda(lq1_ref, lk1_ref, lq2_ref, lk2_ref, lambda_init)
    o_ref[...] = _diff_finish(acc_ref[...], l_ref[...], bq, lam, sub_ref[...], lambda_init).astype(BF16)


def _diff_attention_prompt(lams, subln, dq, dkb, dvb, batch, seq, bq, lambda_init):
    nq = seq // bq
    vec = _const_spec((1, DIFF_DK))
    qspec = pl.BlockSpec((bq, HEAD), lambda b, h, i: (b * nq + i, h))
    kvspec = pl.BlockSpec((seq, HEAD), lambda b, h, i: (b, h))
    return pl.pallas_call(
        functools.partial(_attn_kernel, bq=bq, lambda_init=lambda_init),
        grid=(batch, DIFF_HEADS, nq),
        in_specs=[vec] * 4 + [_const_spec((1, HEAD)), qspec, kvspec, kvspec],
        out_specs=qspec,
        out_shape=jax.ShapeDtypeStruct((batch * seq, SEG), BF16),
        scratch_shapes=[pltpu.VMEM((2 * bq, 1), F32), pltpu.VMEM((2 * bq, 1), F32),
                        pltpu.VMEM((2 * bq, HEAD), F32)],
        compiler_params=_params("parallel", "parallel", "arbitrary"),
        name="diff_attn_prompt",
    )(*lams, subln, dq, dkb, dvb)


def _attn_sample_kernel(lq1_ref, lk1_ref, lq2_ref, lk2_ref, sub_ref, q_ref, kp_ref, vp_ref,
                        kn_ref, vn_ref, o_ref, *, lambda_init):
    n = q_ref.shape[0]
    qs = _stack_heads(q_ref[...])
    s_p = lax.dot_general(qs, kp_ref[...].astype(BF16), _NT, preferred_element_type=F32)
    s_n = lax.dot_general(qs, kn_ref[...], _NT, preferred_element_type=F32)
    m = jnp.maximum(jnp.max(s_p, axis=-1, keepdims=True), jnp.max(s_n, axis=-1, keepdims=True))
    p_p = jnp.exp(s_p - m)
    p_n = jnp.exp(s_n - m)
    l = jnp.sum(p_p, axis=-1, keepdims=True) + jnp.sum(p_n, axis=-1, keepdims=True)
    acc = (jnp.dot(p_p.astype(BF16), vp_ref[...].astype(BF16), preferred_element_type=F32)
           + jnp.dot(p_n.astype(BF16), vn_ref[...], preferred_element_type=F32))
    lam = _lambda(lq1_ref, lk1_ref, lq2_ref, lk2_ref, lambda_init)
    o_ref[...] = _diff_finish(acc, l, n, lam, sub_ref[...], lambda_init).astype(BF16)


def _diff_attention_sample(lams, subln, dq, k_past, v_past, dkb, dvb, batch, seq, past, lambda_init):
    vec = _const_spec((1, DIFF_DK))
    new = pl.BlockSpec((seq, HEAD), lambda b, h: (b, h))
    old = pl.BlockSpec((past, HEAD), lambda b, h: (b, h))
    return pl.pallas_call(
        functools.partial(_attn_sample_kernel, lambda_init=lambda_init),
        grid=(batch, DIFF_HEADS),
        in_specs=[vec] * 4 + [_const_spec((1, HEAD)), new, old, old, new, new],
        out_specs=new,
        out_shape=jax.ShapeDtypeStruct((batch * seq, SEG), BF16),
        compiler_params=_params("parallel", "parallel"),
        name="diff_attn_sample",
    )(*lams, subln, dq, k_past, v_past, dkb, dvb)


def _rms(x, w):
    return x * lax.rsqrt(jnp.mean(x * x, axis=-1, keepdims=True) + NORM_EPS) * w


def _mlp_kernel(ret_ref, dif_ref, x_ref, wo_ref, n1_ref, n2_ref, n3_ref, wu_ref, wd_ref, y_ref, u_ref,
                *, ff_blk):
    mix = (jnp.dot(ret_ref[...], wo_ref[:SEG, :], preferred_element_type=F32)
           + jnp.dot(dif_ref[...], wo_ref[SEG:, :], preferred_element_type=F32))
    x1 = x_ref[...] + _rms(mix, n1_ref[...])
    h = _rms(x1, n2_ref[...]).astype(BF16)
    for c in range(D_FF // ff_blk):
        sl = slice(c * ff_blk, (c + 1) * ff_blk)
        u = jnp.maximum(jnp.dot(h, wu_ref[:, sl], preferred_element_type=F32), 0.0)
        u_ref[:, sl] = (u * u).astype(BF16)
    f = jnp.dot(u_ref[...], wd_ref[...], preferred_element_type=F32)
    y_ref[...] = x1 + _rms(f, n3_ref[...])


def _out_mlp(ret_out, diff_out, x, w_out, n1, n2, n3, w_up, w_down, tm, ff_blk=1024):
    t = x.shape[0]
    half = pl.BlockSpec((tm, SEG), lambda i: (i, 0))
    row = pl.BlockSpec((tm, D_MODEL), lambda i: (i, 0))
    nrm = _const_spec((1, D_MODEL))
    return pl.pallas_call(
        functools.partial(_mlp_kernel, ff_blk=ff_blk),
        grid=(t // tm,),
        in_specs=[half, half, row, _const_spec((D_MODEL, D_MODEL)), nrm, nrm, nrm,
                  _const_spec((D_MODEL, D_FF)), _const_spec((D_FF, D_MODEL))],
        out_specs=row,
        out_shape=jax.ShapeDtypeStruct((t, D_MODEL), F32),
        scratch_shapes=[pltpu.VMEM((tm, D_FF), BF16)],
        compiler_params=_params("parallel"),
        name="out_mlp",
    )(ret_out, diff_out, x, w_out, n1, n2, n3, w_up, w_down)


def _layer(x, pos0, state0, k_past, v_past, w, lambda_init, tm, ret_blk, bq):
    batch, seq, _ = x.shape
    xf = x.reshape(batch * seq, D_MODEL)
    tables = _rope_tables(max(seq, tm), seq, pos0)
    rq, rk, rv, rg, dq, dk, dv, dkb, dvb = _project(xf, w["norm_mix_pre"], w["w_in"], tables, tm)
    ret_out, new_state = _retention(rq, rk, rv, rg, state0, batch, seq, ret_blk)
    lams = (w["lambda_q1"], w["lambda_k1"], w["lambda_q2"], w["lambda_k2"])
    if k_past is None:
        diff_out = _diff_attention_prompt(lams, w["diff_subln"], dq, dkb, dvb, batch, seq, bq, lambda_init)
    else:
        past = k_past.shape[1]
        diff_out = _diff_attention_sample(
            lams, w["diff_subln"], dq, k_past.reshape(batch * past, SEG), v_past.reshape(batch * past, SEG),
            dkb, dvb, batch, seq, past, lambda_init)
    y = _out_mlp(ret_out, diff_out, xf, w["w_out"], w["norm_mix_post"], w["norm_ffn_pre"],
                 w["norm_ffn_post"], w["w_up"], w["w_down"], tm)
    return (y.reshape(batch, seq, D_MODEL),
            dk.reshape(batch, seq, 2 * DIFF_HEADS, DIFF_DK),
            dv.reshape(batch, seq, DIFF_HEADS, HEAD),
            new_state)


def kernel(x_prompt, x_sample, cache_diff_k, cache_diff_v, state_ret, norm_mix_pre, norm_mix_post, w_in, diff_subln, lambda_q1, lambda_k1, lambda_q2, lambda_k2, w_out, norm_ffn_pre, norm_ffn_post, w_up, w_down):
    depth = w_in.shape[0]
    past = cache_diff_k.shape[2]
    yp, ys = x_prompt, x_sample
    outs = [[] for _ in range(6)]
    for l in range(depth):
        lambda_init = 0.8 - 0.6 * math.exp(-0.3 * l)
        w = dict(
            norm_mix_pre=norm_mix_pre[l][None], norm_mix_post=norm_mix_post[l][None],
            w_in=w_in[l].astype(BF16), diff_subln=diff_subln[l][None],
            lambda_q1=lambda_q1[l][None], lambda_k1=lambda_k1[l][None],
            lambda_q2=lambda_q2[l][None], lambda_k2=lambda_k2[l][None],
            w_out=w_out[l].astype(BF16), norm_ffn_pre=norm_ffn_pre[l][None],
            norm_ffn_post=norm_ffn_post[l][None],
            w_up=w_up[l].astype(BF16), w_down=w_down[l].astype(BF16))
        yp, kp, vp, sp = _layer(yp, 0, None, None, None, w, lambda_init,
                                tm=512, ret_blk=CHUNK, bq=256)
        ys, ks, vs, ss = _layer(ys, past, state_ret[l], cache_diff_k[l], cache_diff_v[l], w, lambda_init,
                                tm=x_sample.shape[0] * x_sample.shape[1], ret_blk=x_sample.shape[1], bq=None)
        for lst, val in zip(outs, (kp, vp, sp, ks, vs, ss)):
            lst.append(val)
    return (yp, ys) + tuple(jnp.stack(o) for o in outs)
```

```python
import functools
import math

import jax
import jax.numpy as jnp
from jax import lax
from jax.experimental import pallas as pl
from jax.experimental.pallas import tpu as pltpu

D_MODEL = 1024
CHUNK = 64
HEAD = 128
RET_HEADS = 4
DIFF_HEADS = 4
DIFF_DK = 64
SEG = 512
N_SEG = 7
D_IN = N_SEG * SEG
D_FF = 4 * D_MODEL
ROPE_THETA = 10000.0
NORM_EPS = 1e-6
SUBLN_EPS = 1e-5
NEG_INF = -1e30
ATTN_BLK = 256
RET_BLK = 256
Q_SCALE = DIFF_DK ** -0.5 * math.log2(math.e)
ONES_ROWS = 16
VAUG = HEAD + ONES_ROWS

VMEM_LIMIT_BYTES = 56 * 1024 * 1024

F32 = jnp.float32
BF16 = jnp.bfloat16

_NT = (((1,), (1,)), ((), ()))
_TN = (((0,), (0,)), ((), ()))


def _params(*sem):
    return pltpu.CompilerParams(dimension_semantics=sem, vmem_limit_bytes=VMEM_LIMIT_BYTES)


def _const_spec(shape):
    return pl.BlockSpec(shape, lambda *_: (0,) * len(shape), pipeline_mode=pl.Buffered(1))


def _rope_table_kernel(c128_ref, s128_ref, c64_ref, s64_ref, *, seq, pos0):
    n = c128_ref.shape[0]
    lane = lax.broadcasted_iota(jnp.int32, (n, HEAD), 1)
    pos = (lax.broadcasted_iota(jnp.int32, (n, HEAD), 0) % seq + pos0).astype(F32)
    log_theta = math.log(ROPE_THETA)

    def table(d):
        half = d // 2
        r = lane % d
        i = (r % half).astype(F32)
        inv = jnp.exp(i * (-2.0 * log_theta / d))
        ang = pos * inv
        sign = jnp.where(r < half, -1.0, 1.0).astype(F32)
        return jnp.cos(ang), jnp.sin(ang) * sign

    c128_ref[...], s128_ref[...] = table(HEAD)
    c64_ref[...], s64_ref[...] = table(DIFF_DK)


def _rope_tables(n_rows, seq, pos0):
    shp = jax.ShapeDtypeStruct((n_rows, HEAD), F32)
    return pl.pallas_call(
        functools.partial(_rope_table_kernel, seq=seq, pos0=pos0),
        out_shape=(shp,) * 4,
        name="rope_tables",
    )()


def _proj_kernel(x_ref, nw_ref, w_ref, c128_ref, s128_ref, c64_ref, s64_ref,
                 rq_ref, rk_ref, rv_ref, rg_ref, dq_ref, dk_ref, dv_ref, dkb_ref, dvb_ref, *, transposed):
    x = x_ref[...]
    ms = jnp.mean(x * x, axis=-1, keepdims=True)
    h = (x * lax.rsqrt(ms + NORM_EPS) * nw_ref[...]).astype(BF16)

    def seg(s):
        return jnp.dot(h, w_ref[:, s * SEG:(s + 1) * SEG], preferred_element_type=F32)

    c128, s128 = c128_ref[...], s128_ref[...]
    c64, s64 = c64_ref[...], s64_ref[...]
    tm = x.shape[0]
    lane = lax.broadcasted_iota(jnp.int32, (tm, HEAD), 1)
    first_half64 = (lane % DIFF_DK) < (DIFF_DK // 2)

    def rope128(z):
        return z * c128 + pltpu.roll(z, HEAD // 2, 1) * s128

    def rope64(z):
        rot = jnp.where(first_half64, pltpu.roll(z, HEAD - DIFF_DK // 2, 1),
                        pltpu.roll(z, DIFF_DK // 2, 1))
        return z * c64 + rot * s64

    z = seg(0)
    for hd in range(SEG // HEAD):
        sl = slice(hd * HEAD, (hd + 1) * HEAD)
        rq_ref[:, sl] = rope128(z[:, sl]).astype(BF16)
    z = seg(1)
    for hd in range(SEG // HEAD):
        sl = slice(hd * HEAD, (hd + 1) * HEAD)
        rk_ref[:, sl] = (rope128(z[:, sl]) * (HEAD ** -0.5)).astype(BF16)
    rv_ref[...] = seg(2).astype(BF16)
    rg_ref[...] = seg(3).astype(BF16)
    def store_blocks_t(ref, sl, zt):
        for blk in range(tm // ATTN_BLK):
            ref[blk, sl, :] = zt[:, blk * ATTN_BLK:(blk + 1) * ATTN_BLK].astype(BF16)

    z = seg(4)
    for hd in range(SEG // HEAD):
        sl = slice(hd * HEAD, (hd + 1) * HEAD)
        qr = rope64(z[:, sl]) * Q_SCALE
        if transposed:
            store_blocks_t(dq_ref, sl, qr.T)
        else:
            dq_ref[:, sl] = qr.astype(BF16)
    z = seg(5)
    for hd in range(SEG // HEAD):
        sl = slice(hd * HEAD, (hd + 1) * HEAD)
        kr = rope64(z[:, sl])
        dkb_ref[:, sl] = kr.astype(BF16)
        if transposed:
            dk_ref[0, sl, :] = kr.T
        else:
            dk_ref[:, sl] = kr
    z = seg(6)
    dv_ref[...] = z
    if transposed:
        for hd in range(SEG // HEAD):
            store_blocks_t(dvb_ref, slice(hd * VAUG, hd * VAUG + HEAD), z[:, hd * HEAD:(hd + 1) * HEAD].T)
            dvb_ref[:, hd * VAUG + HEAD:(hd + 1) * VAUG, :] = jnp.ones(
                (tm // ATTN_BLK, ONES_ROWS, ATTN_BLK), BF16)
    else:
        dvb_ref[...] = z.astype(BF16)


def _project(x, norm_w, w_in, tables, tm, seq, transposed):
    t = x.shape[0]
    nt = tables[0].shape[0] // tm
    row = pl.BlockSpec((tm, D_MODEL), lambda i: (i, 0))
    out = pl.BlockSpec((tm, SEG), lambda i: (i, 0))
    tab = pl.BlockSpec((tm, HEAD), lambda i: (i % nt, 0))
    b16 = jax.ShapeDtypeStruct((t, SEG), BF16)
    f32 = jax.ShapeDtypeStruct((t, SEG), F32)
    if transposed:
        per_seq = seq // tm
        def blk_t(rows):
            return (pl.BlockSpec((tm // ATTN_BLK, rows, ATTN_BLK), lambda i: (i, 0, 0)),
                    jax.ShapeDtypeStruct((t // ATTN_BLK, rows, ATTN_BLK), BF16))
        (q_spec, q_shape), (v_spec, v_shape) = blk_t(SEG), blk_t(DIFF_HEADS * VAUG)
        dk_t = pl.BlockSpec((1, SEG, tm), lambda i: (i // per_seq, 0, i % per_seq))
        out_specs = [out] * 4 + [q_spec, dk_t, out, out, v_spec]
        out_shape = [b16] * 4 + [q_shape, jax.ShapeDtypeStruct((t // seq, SEG, seq), F32), f32, b16, v_shape]
    else:
        out_specs = [out] * 9
        out_shape = [b16, b16, b16, b16, b16, f32, f32, b16, b16]
    return pl.pallas_call(
        functools.partial(_proj_kernel, transposed=transposed),
        grid=(t // tm,),
        in_specs=[row, _const_spec((1, D_MODEL)), _const_spec((D_MODEL, D_IN)), tab, tab, tab, tab],
        out_specs=out_specs,
        out_shape=out_shape,
        compiler_params=_params("parallel"),
        name="in_proj",
    )(x, norm_w, w_in, *tables)


def _log_gamma(h):
    return math.log1p(-(2.0 ** (-5.0 - h)))


def _ret_kernel(*refs, blk, has_init):
    if has_init:
        q_ref, k_ref, v_ref, g_ref, s0_ref, o_ref, sout_ref, state, dmat = refs
    else:
        q_ref, k_ref, v_ref, g_ref, o_ref, sout_ref, state, dmat = refs
    b, c = pl.program_id(0), pl.program_id(1)

    @pl.when((b == 0) & (c == 0))
    def _():
        rel = (lax.broadcasted_iota(jnp.int32, (blk, blk), 0)
               - lax.broadcasted_iota(jnp.int32, (blk, blk), 1))
        relf = jnp.maximum(rel, 0).astype(F32)
        for h in range(RET_HEADS):
            dmat[h] = jnp.where(rel >= 0, jnp.exp(_log_gamma(h) * relf), 0.0)

    @pl.when(c == 0)
    def _():
        if has_init:
            state[...] = s0_ref[0]
        else:
            state[...] = jnp.zeros_like(state)

    idx = lax.broadcasted_iota(jnp.int32, (blk, 1), 0).astype(F32)
    for h in range(RET_HEADS):
        sl = slice(h * HEAD, (h + 1) * HEAD)
        lg = _log_gamma(h)
        q, k, v = q_ref[:, sl], k_ref[:, sl], v_ref[:, sl]
        st = state[h]
        s = lax.dot_general(q, k, _NT, preferred_element_type=F32) * dmat[h]
        o = jnp.dot(s.astype(BF16), v, preferred_element_type=F32)
        qd = (q.astype(F32) * jnp.exp(lg * (idx + 1.0))).astype(BF16)
        o = o + jnp.dot(qd, st.astype(BF16), preferred_element_type=F32)
        kd = (k.astype(F32) * jnp.exp(lg * (blk - 1.0 - idx))).astype(BF16)
        state[h] = math.exp(lg * blk) * st + lax.dot_general(kd, v, _TN, preferred_element_type=F32)
        ro = o * lax.rsqrt(jnp.mean(o * o, axis=-1, keepdims=True) + NORM_EPS)
        g = g_ref[:, sl].astype(F32)
        o_ref[:, sl] = (g * (1.0 / (1.0 + jnp.exp(-g))) * ro).astype(BF16)

    @pl.when(c == pl.num_programs(1) - 1)
    def _():
        sout_ref[0] = state[...]


def _retention(rq, rk, rv, rg, state0, batch, seq, blk):
    nc = seq // blk
    tok = pl.BlockSpec((blk, SEG), lambda b, c: (b * nc + c, 0))
    st_spec = pl.BlockSpec((1, RET_HEADS, HEAD, HEAD), lambda b, c: (b, 0, 0, 0))
    has_init = state0 is not None
    args = (rq, rk, rv, rg) + ((state0,) if has_init else ())
    return pl.pallas_call(
        functools.partial(_ret_kernel, blk=blk, has_init=has_init),
        grid=(batch, nc),
        in_specs=[tok] * 4 + ([st_spec] if has_init else []),
        out_specs=[tok, st_spec],
        out_shape=[jax.ShapeDtypeStruct((batch * seq, SEG), BF16),
                   jax.ShapeDtypeStruct((batch, RET_HEADS, HEAD, HEAD), F32)],
        scratch_shapes=[pltpu.VMEM((RET_HEADS, HEAD, HEAD), F32),
                        pltpu.VMEM((RET_HEADS, blk, blk), F32)],
        compiler_params=_params("arbitrary", "arbitrary"),
        name="retention",
    )(*args)


def _lambda(lq1_ref, lk1_ref, lq2_ref, lk2_ref, lambda_init):
    a = jnp.sum(lq1_ref[...] * lk1_ref[...], axis=-1, keepdims=True)
    b = jnp.sum(lq2_ref[...] * lk2_ref[...], axis=-1, keepdims=True)
    return jnp.exp(a) - jnp.exp(b) + lambda_init


def _stack_heads(q):
    lane = lax.broadcasted_iota(jnp.int32, q.shape, 1)
    zero = jnp.zeros_like(q)
    return jnp.concatenate([jnp.where(lane < DIFF_DK, q, zero),
                            jnp.where(lane >= DIFF_DK, q, zero)], axis=0)


def _diff_finish(acc, l, n, lam, sub, lambda_init):
    o = acc[:n] * (1.0 / l[:n]) - lam * (acc[n:] * (1.0 / l[n:]))
    o = o * lax.rsqrt(jnp.mean(o * o, axis=-1, keepdims=True) + SUBLN_EPS)
    return o * sub * (1.0 - lambda_init)


def _attn_kernel(lq1_ref, lk1_ref, lq2_ref, lk2_ref, sub_ref, qt_ref, k_ref, vt_ref, o_ref,
                 qs_ref, m_ref, acc_ref, *, lambda_init):
    blk = ATTN_BLK
    qi = pl.program_id(1)
    row = lax.broadcasted_iota(jnp.int32, (HEAD, blk), 0)
    for h in range(DIFF_HEADS):
        qt = qt_ref[0, h * HEAD:(h + 1) * HEAD, :]
        zero = jnp.zeros_like(qt)
        qs_ref[h, :, :blk] = jnp.where(row < DIFF_DK, qt, zero)
        qs_ref[h, :, blk:] = jnp.where(row >= DIFF_DK, qt, zero)
    m_ref[...] = jnp.full_like(m_ref, NEG_INF)
    acc_ref[...] = jnp.zeros_like(acc_ref)

    def step(j, masked):
        start = pl.multiple_of(j * blk, blk)
        if masked:
            key = lax.broadcasted_iota(jnp.int32, (blk, 2 * blk), 0)
            qry = lax.broadcasted_iota(jnp.int32, (blk, 2 * blk), 1) % blk
            visible = key // CHUNK <= qry // CHUNK

        def scores(h):
            sl = slice(h * HEAD, (h + 1) * HEAD)
            s = jnp.dot(k_ref[pl.ds(start, blk), sl], qs_ref[h], preferred_element_type=F32)
            if masked:
                s = jnp.where(visible, s, NEG_INF)
            m_prev = m_ref[h]
            return s, m_prev, jnp.maximum(m_prev, jnp.max(s, axis=0, keepdims=True))

        def accumulate(h, s, m_prev, m_new):
            alpha = jnp.exp2(m_prev - m_new)
            p = jnp.exp2(s - m_new).astype(BF16)
            pv = jnp.dot(vt_ref[j, h * VAUG:(h + 1) * VAUG, :], p, preferred_element_type=F32)
            acc_ref[h] = alpha * acc_ref[h] + pv
            m_ref[h] = m_new

        pending = scores(0)
        for h in range(DIFF_HEADS):
            current = pending
            if h + 1 < DIFF_HEADS:
                pending = scores(h + 1)
            accumulate(h, *current)

    def body(j, carry):
        step(j, False)
        return carry

    lax.fori_loop(0, qi, body, 0)
    step(qi, True)

    lam = _lambda(lq1_ref, lk1_ref, lq2_ref, lk2_ref, lambda_init)
    for h in range(DIFF_HEADS):
        o = acc_ref[h, :HEAD, :] * (1.0 / acc_ref[h, HEAD:HEAD + 1, :])
        o = o[:, :blk] - lam * o[:, blk:]
        o = o * lax.rsqrt(jnp.mean(o * o, axis=0, keepdims=True) + SUBLN_EPS)
        o = o * sub_ref[...] * (1.0 - lambda_init)
        o_ref[:, h * HEAD:(h + 1) * HEAD] = o.T.astype(BF16)


def _diff_attention_prompt(lams, subln_col, dqt, dkb, dvt, batch, seq, lambda_init):
    blk = ATTN_BLK
    nq = seq // blk
    vec = _const_spec((1, DIFF_DK))
    return pl.pallas_call(
        functools.partial(_attn_kernel, lambda_init=lambda_init),
        grid=(batch, nq),
        in_specs=[vec] * 4 + [_const_spec((HEAD, 1)),
                  pl.BlockSpec((1, SEG, blk), lambda b, i: (b * nq + i, 0, 0)),
                  pl.BlockSpec((seq, SEG), lambda b, i: (b, 0)),
                  pl.BlockSpec((nq, DIFF_HEADS * VAUG, blk), lambda b, i: (b, 0, 0))],
        out_specs=pl.BlockSpec((blk, SEG), lambda b, i: (b * nq + i, 0)),
        out_shape=jax.ShapeDtypeStruct((batch * seq, SEG), BF16),
        scratch_shapes=[pltpu.VMEM((DIFF_HEADS, HEAD, 2 * blk), BF16),
                        pltpu.VMEM((DIFF_HEADS, 1, 2 * blk), F32),
                        pltpu.VMEM((DIFF_HEADS, VAUG, 2 * blk), F32)],
        compiler_params=_params("parallel", "arbitrary"),
        name="diff_attn_prompt",
    )(*lams, subln_col, dqt, dkb, dvt)


def _attn_sample_kernel(lq1_ref, lk1_ref, lq2_ref, lk2_ref, sub_ref, q_ref, kp_ref, vp_ref,
                        kn_ref, vn_ref, o_ref, *, lambda_init):
    n = q_ref.shape[0]
    qs = _stack_heads(q_ref[...])
    s_p = lax.dot_general(qs, kp_ref[...].astype(BF16), _NT, preferred_element_type=F32)
    s_n = lax.dot_general(qs, kn_ref[...], _NT, preferred_element_type=F32)
    m = jnp.maximum(jnp.max(s_p, axis=-1, keepdims=True), jnp.max(s_n, axis=-1, keepdims=True))
    p_p = jnp.exp2(s_p - m)
    p_n = jnp.exp2(s_n - m)
    l = jnp.sum(p_p, axis=-1, keepdims=True) + jnp.sum(p_n, axis=-1, keepdims=True)
    acc = (jnp.dot(p_p.astype(BF16), vp_ref[...].astype(BF16), preferred_element_type=F32)
           + jnp.dot(p_n.astype(BF16), vn_ref[...], preferred_element_type=F32))
    lam = _lambda(lq1_ref, lk1_ref, lq2_ref, lk2_ref, lambda_init)
    o_ref[...] = _diff_finish(acc, l, n, lam, sub_ref[...], lambda_init).astype(BF16)


def _diff_attention_sample(lams, subln, dq, k_past, v_past, dkb, dvb, batch, seq, past, lambda_init):
    vec = _const_spec((1, DIFF_DK))
    new = pl.BlockSpec((seq, HEAD), lambda b, h: (b, h))
    old = pl.BlockSpec((past, HEAD), lambda b, h: (b, h))
    return pl.pallas_call(
        functools.partial(_attn_sample_kernel, lambda_init=lambda_init),
        grid=(batch, DIFF_HEADS),
        in_specs=[vec] * 4 + [_const_spec((1, HEAD)), new, old, old, new, new],
        out_specs=new,
        out_shape=jax.ShapeDtypeStruct((batch * seq, SEG), BF16),
        compiler_params=_params("parallel", "parallel"),
        name="diff_attn_sample",
    )(*lams, subln, dq, k_past, v_past, dkb, dvb)


def _rms(x, w):
    return x * lax.rsqrt(jnp.mean(x * x, axis=-1, keepdims=True) + NORM_EPS) * w


def _mlp_kernel(ret_ref, dif_ref, x_ref, wo_ref, n1_ref, n2_ref, n3_ref, wu_ref, wd_ref, y_ref, u_ref,
                *, ff_blk):
    mix = (jnp.dot(ret_ref[...], wo_ref[:SEG, :], preferred_element_type=F32)
           + jnp.dot(dif_ref[...], wo_ref[SEG:, :], preferred_element_type=F32))
    x1 = x_ref[...] + _rms(mix, n1_ref[...])
    h = _rms(x1, n2_ref[...]).astype(BF16)
    for c in range(D_FF // ff_blk):
        sl = slice(c * ff_blk, (c + 1) * ff_blk)
        u = jnp.maximum(jnp.dot(h, wu_ref[:, sl], preferred_element_type=F32), 0.0)
        u_ref[:, sl] = (u * u).astype(BF16)
    f = jnp.dot(u_ref[...], wd_ref[...], preferred_element_type=F32)
    y_ref[...] = x1 + _rms(f, n3_ref[...])


def _out_mlp(ret_out, diff_out, x, w_out, n1, n2, n3, w_up, w_down, tm, ff_blk=1024):
    t = x.shape[0]
    half = pl.BlockSpec((tm, SEG), lambda i: (i, 0))
    row = pl.BlockSpec((tm, D_MODEL), lambda i: (i, 0))
    nrm = _const_spec((1, D_MODEL))
    return pl.pallas_call(
        functools.partial(_mlp_kernel, ff_blk=ff_blk),
        grid=(t // tm,),
        in_specs=[half, half, row, _const_spec((D_MODEL, D_MODEL)), nrm, nrm, nrm,
                  _const_spec((D_MODEL, D_FF)), _const_spec((D_FF, D_MODEL))],
        out_specs=row,
        out_shape=jax.ShapeDtypeStruct((t, D_MODEL), F32),
        scratch_shapes=[pltpu.VMEM((tm, D_FF), BF16)],
        compiler_params=_params("parallel"),
        name="out_mlp",
    )(ret_out, diff_out, x, w_out, n1, n2, n3, w_up, w_down)


def _layer(x, pos0, state0, k_past, v_past, w, lambda_init, tm, ret_blk):
    batch, seq, _ = x.shape
    prompt = k_past is None
    xf = x.reshape(batch * seq, D_MODEL)
    tables = _rope_tables(max(seq, tm), seq, pos0)
    rq, rk, rv, rg, dq, dk, dv, dkb, dvb = _project(xf, w["norm_mix_pre"], w["w_in"], tables, tm, seq,
                                                    transposed=prompt)
    ret_out, new_state = _retention(rq, rk, rv, rg, state0, batch, seq, ret_blk)
    lams = (w["lambda_q1"], w["lambda_k1"], w["lambda_q2"], w["lambda_k2"])
    if prompt:
        diff_out = _diff_attention_prompt(lams, w["diff_subln"].reshape(HEAD, 1), dq, dkb, dvb,
                                          batch, seq, lambda_init)
        k_new = dk.reshape(batch, 2 * DIFF_HEADS, DIFF_DK, seq).transpose(0, 3, 1, 2)
    else:
        past = k_past.shape[1]
        diff_out = _diff_attention_sample(
            lams, w["diff_subln"], dq, k_past.reshape(batch * past, SEG), v_past.reshape(batch * past, SEG),
            dkb, dvb, batch, seq, past, lambda_init)
        k_new = dk.reshape(batch, seq, 2 * DIFF_HEADS, DIFF_DK)
    y = _out_mlp(ret_out, diff_out, xf, w["w_out"], w["norm_mix_post"], w["norm_ffn_pre"],
                 w["norm_ffn_post"], w["w_up"], w["w_down"], tm)
    return (y.reshape(batch, seq, D_MODEL), k_new, dv.reshape(batch, seq, DIFF_HEADS, HEAD), new_state)


def kernel(x_prompt, x_sample, cache_diff_k, cache_diff_v, state_ret, norm_mix_pre, norm_mix_post, w_in, diff_subln, lambda_q1, lambda_k1, lambda_q2, lambda_k2, w_out, norm_ffn_pre, norm_ffn_post, w_up, w_down):
    depth = w_in.shape[0]
    past = cache_diff_k.shape[2]
    yp, ys = x_prompt, x_sample
    outs = [[] for _ in range(6)]
    for l in range(depth):
        lambda_init = 0.8 - 0.6 * math.exp(-0.3 * l)
        w = dict(
            norm_mix_pre=norm_mix_pre[l][None], norm_mix_post=norm_mix_post[l][None],
            w_in=w_in[l].astype(BF16), diff_subln=diff_subln[l][None],
            lambda_q1=lambda_q1[l][None], lambda_k1=lambda_k1[l][None],
            lambda_q2=lambda_q2[l][None], lambda_k2=lambda_k2[l][None],
            w_out=w_out[l].astype(BF16), norm_ffn_pre=norm_ffn_pre[l][None],
            norm_ffn_post=norm_ffn_post[l][None],
            w_up=w_up[l].astype(BF16), w_down=w_down[l].astype(BF16))
        yp, kp, vp, sp = _layer(yp, 0, None, None, None, w, lambda_init,
                                tm=512, ret_blk=RET_BLK)
        ys, ks, vs, ss = _layer(ys, past, state_ret[l], cache_diff_k[l], cache_diff_v[l], w, lambda_init,
                                tm=x_sample.shape[0] * x_sample.shape[1], ret_blk=x_sample.shape[1])
        for lst, val in zip(outs, (kp, vp, sp, ks, vs, ss)):
            lst.append(val)
    return (yp, ys) + tuple(jnp.stack(o) for o in outs)
```

```python
import functools
import math

import jax
import jax.numpy as jnp
from jax import lax
from jax.experimental import pallas as pl
from jax.experimental.pallas import tpu as pltpu

D_MODEL = 1024
CHUNK = 64
HEAD = 128
RET_HEADS = 4
DIFF_HEADS = 4
DIFF_DK = 64
SEG = 512
N_SEG = 7
D_IN = N_SEG * SEG
D_FF = 4 * D_MODEL
ROPE_THETA = 10000.0
NORM_EPS = 1e-6
SUBLN_EPS = 1e-5
NEG_INF = -1e30
ATTN_BLK = 256
RET_BLK = 256
Q_SCALE = DIFF_DK ** -0.5 * math.log2(math.e)
ONES_ROWS = 16
VAUG = HEAD + ONES_ROWS
SKEW = 2

VMEM_LIMIT_BYTES = 56 * 1024 * 1024

F32 = jnp.float32
BF16 = jnp.bfloat16

_NT = (((1,), (1,)), ((), ()))
_TN = (((0,), (0,)), ((), ()))


def _params(*sem):
    return pltpu.CompilerParams(dimension_semantics=sem, vmem_limit_bytes=VMEM_LIMIT_BYTES)


def _const_spec(shape):
    return pl.BlockSpec(shape, lambda *_: (0,) * len(shape), pipeline_mode=pl.Buffered(1))


def _rope_table_kernel(c128_ref, s128_ref, c64_ref, s64_ref, *, seq, pos0):
    n = c128_ref.shape[0]
    lane = lax.broadcasted_iota(jnp.int32, (n, HEAD), 1)
    pos = (lax.broadcasted_iota(jnp.int32, (n, HEAD), 0) % seq + pos0).astype(F32)
    log_theta = math.log(ROPE_THETA)

    def table(d):
        half = d // 2
        r = lane % d
        i = (r % half).astype(F32)
        inv = jnp.exp(i * (-2.0 * log_theta / d))
        ang = pos * inv
        sign = jnp.where(r < half, -1.0, 1.0).astype(F32)
        return jnp.cos(ang), jnp.sin(ang) * sign

    c128_ref[...], s128_ref[...] = table(HEAD)
    c64_ref[...], s64_ref[...] = table(DIFF_DK)


def _rope_tables(n_rows, seq, pos0):
    shp = jax.ShapeDtypeStruct((n_rows, HEAD), F32)
    return pl.pallas_call(
        functools.partial(_rope_table_kernel, seq=seq, pos0=pos0),
        out_shape=(shp,) * 4,
        name="rope_tables",
    )()


def _proj_kernel(x_ref, nw_ref, w_ref, c128_ref, s128_ref, c64_ref, s64_ref,
                 rq_ref, rk_ref, rv_ref, rg_ref, dq_ref, dk_ref, dv_ref, dkb_ref, dvb_ref, *, transposed):
    x = x_ref[...]
    ms = jnp.mean(x * x, axis=-1, keepdims=True)
    h = (x * lax.rsqrt(ms + NORM_EPS) * nw_ref[...]).astype(BF16)

    def seg(s):
        return jnp.dot(h, w_ref[:, s * SEG:(s + 1) * SEG], preferred_element_type=F32)

    c128, s128 = c128_ref[...], s128_ref[...]
    c64, s64 = c64_ref[...], s64_ref[...]
    tm = x.shape[0]
    lane = lax.broadcasted_iota(jnp.int32, (tm, HEAD), 1)
    first_half64 = (lane % DIFF_DK) < (DIFF_DK // 2)

    def rope128(z):
        return z * c128 + pltpu.roll(z, HEAD // 2, 1) * s128

    def rope64(z):
        rot = jnp.where(first_half64, pltpu.roll(z, HEAD - DIFF_DK // 2, 1),
                        pltpu.roll(z, DIFF_DK // 2, 1))
        return z * c64 + rot * s64

    z = seg(0)
    for hd in range(SEG // HEAD):
        sl = slice(hd * HEAD, (hd + 1) * HEAD)
        rq_ref[:, sl] = rope128(z[:, sl]).astype(BF16)
    def store_blocks_t(ref, sl, zt, width):
        for blk in range(tm // width):
            ref[blk, sl, :] = zt[:, blk * width:(blk + 1) * width].astype(BF16)

    z = seg(1)
    for hd in range(SEG // HEAD):
        sl = slice(hd * HEAD, (hd + 1) * HEAD)
        kr = rope128(z[:, sl]) * (HEAD ** -0.5)
        if transposed:
            store_blocks_t(rk_ref, sl, kr.T, RET_BLK)
        else:
            rk_ref[:, sl] = kr.astype(BF16)
    rv_ref[...] = seg(2).astype(BF16)
    rg_ref[...] = seg(3).astype(BF16)
    z = seg(4)
    for hd in range(SEG // HEAD):
        sl = slice(hd * HEAD, (hd + 1) * HEAD)
        qr = rope64(z[:, sl]) * Q_SCALE
        if transposed:
            store_blocks_t(dq_ref, sl, qr.T, ATTN_BLK)
        else:
            dq_ref[:, sl] = qr.astype(BF16)
    z = seg(5)
    for hd in range(SEG // HEAD):
        sl = slice(hd * HEAD, (hd + 1) * HEAD)
        kr = rope64(z[:, sl])
        dkb_ref[:, sl] = kr.astype(BF16)
        if transposed:
            dk_ref[0, sl, :] = kr.T
        else:
            dk_ref[:, sl] = kr
    z = seg(6)
    for hd in range(DIFF_HEADS):
        dv_ref[pl.ds(hd, tm, stride=DIFF_HEADS), :] = z[:, hd * HEAD:(hd + 1) * HEAD]
    if transposed:
        for hd in range(SEG // HEAD):
            store_blocks_t(dvb_ref, slice(hd * VAUG, hd * VAUG + HEAD), z[:, hd * HEAD:(hd + 1) * HEAD].T,
                           ATTN_BLK)
            dvb_ref[:, hd * VAUG + HEAD:(hd + 1) * VAUG, :] = jnp.ones(
                (tm // ATTN_BLK, ONES_ROWS, ATTN_BLK), BF16)
    else:
        dvb_ref[...] = z.astype(BF16)


def _project(x, norm_w, w_in, tables, tm, seq, transposed):
    t = x.shape[0]
    nt = tables[0].shape[0] // tm
    row = pl.BlockSpec((tm, D_MODEL), lambda i: (i, 0))
    out = pl.BlockSpec((tm, SEG), lambda i: (i, 0))
    tab = pl.BlockSpec((tm, HEAD), lambda i: (i % nt, 0))
    b16 = jax.ShapeDtypeStruct((t, SEG), BF16)
    f32 = jax.ShapeDtypeStruct((t, SEG), F32)
    dv_spec = pl.BlockSpec((tm * DIFF_HEADS, HEAD), lambda i: (i, 0))
    dv_shape = jax.ShapeDtypeStruct((t * DIFF_HEADS, HEAD), F32)
    if transposed:
        per_seq = seq // tm
        def blk_t(rows, width):
            return (pl.BlockSpec((tm // width, rows, width), lambda i: (i, 0, 0)),
                    jax.ShapeDtypeStruct((t // width, rows, width), BF16))
        (rk_spec, rk_shape), (q_spec, q_shape) = blk_t(SEG, RET_BLK), blk_t(SEG, ATTN_BLK)
        v_spec, v_shape = blk_t(DIFF_HEADS * VAUG, ATTN_BLK)
        dk_t = pl.BlockSpec((1, SEG, tm), lambda i: (i // per_seq, 0, i % per_seq))
        out_specs = [out, rk_spec, out, out, q_spec, dk_t, dv_spec, out, v_spec]
        out_shape = [b16, rk_shape, b16, b16, q_shape, jax.ShapeDtypeStruct((t // seq, SEG, seq), F32),
                     dv_shape, b16, v_shape]
    else:
        out_specs = [out] * 6 + [dv_spec, out, out]
        out_shape = [b16, b16, b16, b16, b16, f32, dv_shape, b16, b16]
    return pl.pallas_call(
        functools.partial(_proj_kernel, transposed=transposed),
        grid=(t // tm,),
        in_specs=[row, _const_spec((1, D_MODEL)), _const_spec((D_MODEL, D_IN)), tab, tab, tab, tab],
        out_specs=out_specs,
        out_shape=out_shape,
        compiler_params=_params("parallel"),
        name="in_proj",
    )(x, norm_w, w_in, *tables)


def _log_gamma(h):
    return math.log1p(-(2.0 ** (-5.0 - h)))


def _ret_kernel(*refs, blk, has_init, k_feature_major):
    if has_init:
        q_ref, k_ref, v_ref, g_ref, s0_ref, o_ref, sout_ref, state, dmat, qdec, kdec = refs
    else:
        q_ref, k_ref, v_ref, g_ref, o_ref, sout_ref, state, dmat, qdec, kdec = refs
    b, c = pl.program_id(0), pl.program_id(1)

    @pl.when((b == 0) & (c == 0))
    def _():
        rel = (lax.broadcasted_iota(jnp.int32, (blk, blk), 0)
               - lax.broadcasted_iota(jnp.int32, (blk, blk), 1))
        relf = jnp.maximum(rel, 0).astype(F32)
        row = lax.broadcasted_iota(jnp.int32, qdec.shape[1:], 0).astype(F32)
        along = lax.broadcasted_iota(jnp.int32, kdec.shape[1:], 1 if k_feature_major else 0).astype(F32)
        for h in range(RET_HEADS):
            lg = _log_gamma(h)
            dmat[h] = jnp.where(rel >= 0, jnp.exp(lg * relf), 0.0)
            qdec[h] = jnp.exp(lg * (row + 1.0))
            kdec[h] = jnp.exp(lg * (blk - 1.0 - along))

    @pl.when(c == 0)
    def _():
        if has_init:
            state[...] = s0_ref[0]
        else:
            state[...] = jnp.zeros_like(state)

    for h in range(RET_HEADS):
        sl = slice(h * HEAD, (h + 1) * HEAD)
        q, v = q_ref[:, sl], v_ref[:, sl]
        st = state[h]
        if k_feature_major:
            kt = k_ref[0, sl, :]
            s = jnp.dot(q, kt, preferred_element_type=F32)
            kd = (kt.astype(F32) * kdec[h]).astype(BF16)
            upd = jnp.dot(kd, v, preferred_element_type=F32)
        else:
            k = k_ref[:, sl]
            s = lax.dot_general(q, k, _NT, preferred_element_type=F32)
            kd = (k.astype(F32) * kdec[h]).astype(BF16)
            upd = lax.dot_general(kd, v, _TN, preferred_element_type=F32)
        o = jnp.dot((s * dmat[h]).astype(BF16), v, preferred_element_type=F32)
        o = o + jnp.dot(q, st.astype(BF16), preferred_element_type=F32) * qdec[h]
        state[h] = math.exp(_log_gamma(h) * blk) * st + upd
        ro = o * lax.rsqrt(jnp.mean(o * o, axis=-1, keepdims=True) + NORM_EPS)
        g = g_ref[:, sl].astype(F32)
        o_ref[:, sl] = (g * (1.0 / (1.0 + jnp.exp(-g))) * ro).astype(BF16)

    @pl.when(c == pl.num_programs(1) - 1)
    def _():
        sout_ref[0] = state[...]


def _retention(rq, rk, rv, rg, state0, batch, seq, blk):
    nc = seq // blk
    tok = pl.BlockSpec((blk, SEG), lambda b, c: (b * nc + c, 0))
    st_spec = pl.BlockSpec((1, RET_HEADS, HEAD, HEAD), lambda b, c: (b, 0, 0, 0))
    has_init = state0 is not None
    k_feature_major = rk.ndim == 3
    k_spec = pl.BlockSpec((1, SEG, blk), lambda b, c: (b * nc + c, 0, 0)) if k_feature_major else tok
    args = (rq, rk, rv, rg) + ((state0,) if has_init else ())
    return pl.pallas_call(
        functools.partial(_ret_kernel, blk=blk, has_init=has_init, k_feature_major=k_feature_major),
        grid=(batch, nc),
        in_specs=[tok, k_spec, tok, tok] + ([st_spec] if has_init else []),
        out_specs=[tok, st_spec],
        out_shape=[jax.ShapeDtypeStruct((batch * seq, SEG), BF16),
                   jax.ShapeDtypeStruct((batch, RET_HEADS, HEAD, HEAD), F32)],
        scratch_shapes=[pltpu.VMEM((RET_HEADS, HEAD, HEAD), F32),
                        pltpu.VMEM((RET_HEADS, blk, blk), F32),
                        pltpu.VMEM((RET_HEADS, blk, HEAD), F32),
                        pltpu.VMEM((RET_HEADS, 1, blk) if k_feature_major else (RET_HEADS, blk, HEAD), F32)],
        compiler_params=_params("arbitrary", "arbitrary"),
        name="retention",
    )(*args)


def _lambda(lq1_ref, lk1_ref, lq2_ref, lk2_ref, lambda_init):
    a = jnp.sum(lq1_ref[...] * lk1_ref[...], axis=-1, keepdims=True)
    b = jnp.sum(lq2_ref[...] * lk2_ref[...], axis=-1, keepdims=True)
    return jnp.exp(a) - jnp.exp(b) + lambda_init


def _stack_heads(q):
    lane = lax.broadcasted_iota(jnp.int32, q.shape, 1)
    zero = jnp.zeros_like(q)
    return jnp.concatenate([jnp.where(lane < DIFF_DK, q, zero),
                            jnp.where(lane >= DIFF_DK, q, zero)], axis=0)


def _diff_finish(acc, l, n, lam, sub, lambda_init):
    o = acc[:n] * (1.0 / l[:n]) - lam * (acc[n:] * (1.0 / l[n:]))
    o = o * lax.rsqrt(jnp.mean(o * o, axis=-1, keepdims=True) + SUBLN_EPS)
    return o * sub * (1.0 - lambda_init)


def _attn_kernel(lq1_ref, lk1_ref, lq2_ref, lk2_ref, sub_ref, qt_ref, k_ref, vt_ref, o_ref,
                 qs_ref, m_ref, acc_ref, *, lambda_init):
    blk = ATTN_BLK
    qi = pl.program_id(1)
    row = lax.broadcasted_iota(jnp.int32, (HEAD, blk), 0)
    for h in range(DIFF_HEADS):
        qt = qt_ref[0, h * HEAD:(h + 1) * HEAD, :]
        zero = jnp.zeros_like(qt)
        qs_ref[h, :, :blk] = jnp.where(row < DIFF_DK, qt, zero)
        qs_ref[h, :, blk:] = jnp.where(row >= DIFF_DK, qt, zero)
    m_ref[...] = jnp.full_like(m_ref, NEG_INF)
    acc_ref[...] = jnp.zeros_like(acc_ref)

    def run(blocks):
        items = [(j, masked, h) for j, masked in blocks for h in range(DIFF_HEADS)]
        m_cur = [m_ref[h] for h in range(DIFF_HEADS)]
        if any(masked for _, masked in blocks):
            key = lax.broadcasted_iota(jnp.int32, (blk, 2 * blk), 0)
            qry = lax.broadcasted_iota(jnp.int32, (blk, 2 * blk), 1) % blk
            visible = key // CHUNK <= qry // CHUNK

        def scores(j, masked, h):
            start = pl.multiple_of(j * blk, blk)
            s = jnp.dot(k_ref[pl.ds(start, blk), h * HEAD:(h + 1) * HEAD], qs_ref[h],
                        preferred_element_type=F32)
            if masked:
                s = jnp.where(visible, s, NEG_INF)
            m_prev = m_cur[h]
            m_cur[h] = jnp.maximum(m_prev, jnp.max(s, axis=0, keepdims=True))
            return j, h, s, m_prev, m_cur[h]

        def accumulate(j, h, s, m_prev, m_new):
            alpha = jnp.exp2(m_prev - m_new)
            p = jnp.exp2(s - m_new).astype(BF16)
            pv = jnp.dot(vt_ref[j, h * VAUG:(h + 1) * VAUG, :], p, preferred_element_type=F32)
            acc_ref[h] = alpha * acc_ref[h] + pv

        pending = [scores(*it) for it in items[:SKEW]]
        for i in range(len(items)):
            current = pending.pop(0)
            if i + SKEW < len(items):
                pending.append(scores(*items[i + SKEW]))
            accumulate(*current)
        for h in range(DIFF_HEADS):
            m_ref[h] = m_cur[h]

    def body(t, carry):
        run([(2 * t, False), (2 * t + 1, False)])
        return carry

    lax.fori_loop(0, qi // 2, body, 0)

    @pl.when(qi % 2 == 1)
    def _():
        run([(qi - 1, False), (qi, True)])

    @pl.when(qi % 2 == 0)
    def _():
        run([(qi, True)])

    lam = _lambda(lq1_ref, lk1_ref, lq2_ref, lk2_ref, lambda_init)
    for h in range(DIFF_HEADS):
        o = acc_ref[h, :HEAD, :] * (1.0 / acc_ref[h, HEAD:HEAD + 1, :])
        o = o[:, :blk] - lam * o[:, blk:]
        o = o * lax.rsqrt(jnp.mean(o * o, axis=0, keepdims=True) + SUBLN_EPS)
        o = o * sub_ref[...] * (1.0 - lambda_init)
        o_ref[:, h * HEAD:(h + 1) * HEAD] = o.T.astype(BF16)


def _diff_attention_prompt(lams, subln_col, dqt, dkb, dvt, batch, seq, lambda_init):
    blk = ATTN_BLK
    nq = seq // blk
    vec = _const_spec((1, DIFF_DK))
    return pl.pallas_call(
        functools.partial(_attn_kernel, lambda_init=lambda_init),
        grid=(batch, nq),
        in_specs=[vec] * 4 + [_const_spec((HEAD, 1)),
                  pl.BlockSpec((1, SEG, blk), lambda b, i: (b * nq + i, 0, 0)),
                  pl.BlockSpec((seq, SEG), lambda b, i: (b, 0)),
                  pl.BlockSpec((nq, DIFF_HEADS * VAUG, blk), lambda b, i: (b, 0, 0))],
        out_specs=pl.BlockSpec((blk, SEG), lambda b, i: (b * nq + i, 0)),
        out_shape=jax.ShapeDtypeStruct((batch * seq, SEG), BF16),
        scratch_shapes=[pltpu.VMEM((DIFF_HEADS, HEAD, 2 * blk), BF16),
                        pltpu.VMEM((DIFF_HEADS, 1, 2 * blk), F32),
                        pltpu.VMEM((DIFF_HEADS, VAUG, 2 * blk), F32)],
        compiler_params=_params("parallel", "arbitrary"),
        name="diff_attn_prompt",
    )(*lams, subln_col, dqt, dkb, dvt)


def _attn_sample_kernel(lq1_ref, lk1_ref, lq2_ref, lk2_ref, sub_ref, q_ref, kpt_ref, vp_ref,
                        kn_ref, vn_ref, o_ref, *, lambda_init):
    n = q_ref.shape[0]
    past = kpt_ref.shape[2]
    lam = _lambda(lq1_ref, lk1_ref, lq2_ref, lk2_ref, lambda_init)
    for h in range(DIFF_HEADS):
        sl = slice(h * HEAD, (h + 1) * HEAD)
        qs = _stack_heads(q_ref[:, sl])
        s_p = jnp.dot(qs, kpt_ref[0, sl, :].astype(BF16), preferred_element_type=F32)
        s_n = lax.dot_general(qs, kn_ref[:, sl], _NT, preferred_element_type=F32)
        m = jnp.maximum(jnp.max(s_p, axis=-1, keepdims=True), jnp.max(s_n, axis=-1, keepdims=True))
        p_p = jnp.exp2(s_p - m)
        p_n = jnp.exp2(s_n - m)
        l = jnp.sum(p_p, axis=-1, keepdims=True) + jnp.sum(p_n, axis=-1, keepdims=True)
        v_p = vp_ref[0, pl.ds(h, past, stride=DIFF_HEADS), :].astype(BF16)
        acc = (jnp.dot(p_p.astype(BF16), v_p, preferred_element_type=F32)
               + jnp.dot(p_n.astype(BF16), vn_ref[:, sl], preferred_element_type=F32))
        o_ref[:, sl] = _diff_finish(acc, l, n, lam, sub_ref[...], lambda_init).astype(BF16)


def _diff_attention_sample(lams, subln, dq, kt_past, v_past, dkb, dvb, batch, seq, lambda_init):
    past = kt_past.shape[2]
    vec = _const_spec((1, DIFF_DK))
    new = pl.BlockSpec((seq, SEG), lambda b: (b, 0))
    return pl.pallas_call(
        functools.partial(_attn_sample_kernel, lambda_init=lambda_init),
        grid=(batch,),
        in_specs=[vec] * 4 + [_const_spec((1, HEAD)), new,
                  pl.BlockSpec((1, SEG, past), lambda b: (b, 0, 0)),
                  pl.BlockSpec((1, DIFF_HEADS * past, HEAD), lambda b: (b, 0, 0)), new, new],
        out_specs=new,
        out_shape=jax.ShapeDtypeStruct((batch * seq, SEG), BF16),
        compiler_params=_params("parallel"),
        name="diff_attn_sample",
    )(*lams, subln, dq, kt_past, v_past, dkb, dvb)


def _rms(x, w):
    return x * lax.rsqrt(jnp.mean(x * x, axis=-1, keepdims=True) + NORM_EPS) * w


def _mlp_kernel(ret_ref, dif_ref, x_ref, wo_ref, n1_ref, n2_ref, n3_ref, wu_ref, wd_ref, y_ref, u_ref,
                *, ff_blk):
    mix = (jnp.dot(ret_ref[...], wo_ref[:SEG, :], preferred_element_type=F32)
           + jnp.dot(dif_ref[...], wo_ref[SEG:, :], preferred_element_type=F32))
    x1 = x_ref[...] + _rms(mix, n1_ref[...])
    h = _rms(x1, n2_ref[...]).astype(BF16)
    for c in range(D_FF // ff_blk):
        sl = slice(c * ff_blk, (c + 1) * ff_blk)
        u = jnp.maximum(jnp.dot(h, wu_ref[:, sl], preferred_element_type=F32), 0.0)
        u_ref[:, sl] = (u * u).astype(BF16)
    f = jnp.dot(u_ref[...], wd_ref[...], preferred_element_type=F32)
    y_ref[...] = x1 + _rms(f, n3_ref[...])


def _out_mlp(ret_out, diff_out, x, w_out, n1, n2, n3, w_up, w_down, tm, ff_blk=1024):
    t = x.shape[0]
    half = pl.BlockSpec((tm, SEG), lambda i: (i, 0))
    row = pl.BlockSpec((tm, D_MODEL), lambda i: (i, 0))
    nrm = _const_spec((1, D_MODEL))
    return pl.pallas_call(
        functools.partial(_mlp_kernel, ff_blk=ff_blk),
        grid=(t // tm,),
        in_specs=[half, half, row, _const_spec((D_MODEL, D_MODEL)), nrm, nrm, nrm,
                  _const_spec((D_MODEL, D_FF)), _const_spec((D_FF, D_MODEL))],
        out_specs=row,
        out_shape=jax.ShapeDtypeStruct((t, D_MODEL), F32),
        scratch_shapes=[pltpu.VMEM((tm, D_FF), BF16)],
        compiler_params=_params("parallel"),
        name="out_mlp",
    )(ret_out, diff_out, x, w_out, n1, n2, n3, w_up, w_down)


def _layer(x, pos0, state0, k_past, v_past, w, lambda_init, tm, ret_blk):
    batch, seq, _ = x.shape
    prompt = k_past is None
    xf = x.reshape(batch * seq, D_MODEL)
    tables = _rope_tables(max(seq, tm), seq, pos0)
    rq, rk, rv, rg, dq, dk, dv, dkb, dvb = _project(xf, w["norm_mix_pre"], w["w_in"], tables, tm, seq,
                                                    transposed=prompt)
    ret_out, new_state = _retention(rq, rk, rv, rg, state0, batch, seq, ret_blk)
    lams = (w["lambda_q1"], w["lambda_k1"], w["lambda_q2"], w["lambda_k2"])
    if prompt:
        diff_out = _diff_attention_prompt(lams, w["diff_subln"].reshape(HEAD, 1), dq, dkb, dvb,
                                          batch, seq, lambda_init)
        k_new = dk.reshape(batch, 2 * DIFF_HEADS, DIFF_DK, seq).transpose(0, 3, 1, 2)
    else:
        past = k_past.shape[1]
        kt_past = k_past.transpose(0, 2, 3, 1).reshape(batch, SEG, past)
        diff_out = _diff_attention_sample(
            lams, w["diff_subln"], dq, kt_past, v_past.reshape(batch, past * DIFF_HEADS, HEAD),
            dkb, dvb, batch, seq, lambda_init)
        k_new = dk.reshape(batch, seq, 2 * DIFF_HEADS, DIFF_DK)
    y = _out_mlp(ret_out, diff_out, xf, w["w_out"], w["norm_mix_post"], w["norm_ffn_pre"],
                 w["norm_ffn_post"], w["w_up"], w["w_down"], tm)
    return (y.reshape(batch, seq, D_MODEL), k_new, dv.reshape(batch, seq, DIFF_HEADS, HEAD), new_state)


def kernel(x_prompt, x_sample, cache_diff_k, cache_diff_v, state_ret, norm_mix_pre, norm_mix_post, w_in, diff_subln, lambda_q1, lambda_k1, lambda_q2, lambda_k2, w_out, norm_ffn_pre, norm_ffn_post, w_up, w_down):
    depth = w_in.shape[0]
    past = cache_diff_k.shape[2]
    yp, ys = x_prompt, x_sample
    outs = [[] for _ in range(6)]
    for l in range(depth):
        lambda_init = 0.8 - 0.6 * math.exp(-0.3 * l)
        w = dict(
            norm_mix_pre=norm_mix_pre[l][None], norm_mix_post=norm_mix_post[l][None],
            w_in=w_in[l].astype(BF16), diff_subln=diff_subln[l][None],
            lambda_q1=lambda_q1[l][None], lambda_k1=lambda_k1[l][None],
            lambda_q2=lambda_q2[l][None], lambda_k2=lambda_k2[l][None],
            w_out=w_out[l].astype(BF16), norm_ffn_pre=norm_ffn_pre[l][None],
            norm_ffn_post=norm_ffn_post[l][None],
            w_up=w_up[l].astype(BF16), w_down=w_down[l].astype(BF16))
        yp, kp, vp, sp = _layer(yp, 0, None, None, None, w, lambda_init,
                                tm=512, ret_blk=RET_BLK)
        ys, ks, vs, ss = _layer(ys, past, state_ret[l], cache_diff_k[l], cache_diff_v[l], w, lambda_init,
                                tm=x_sample.shape[0] * x_sample.shape[1], ret_blk=x_sample.shape[1])
        for lst, val in zip(outs, (kp, vp, sp, ks, vs, ss)):
            lst.append(val)
    return (yp, ys) + tuple(jnp.stack(o) for o in outs)
```

```python
import functools
import math

import jax
import jax.numpy as jnp
from jax import lax
from jax.experimental import pallas as pl
from jax.experimental.pallas import tpu as pltpu

D_MODEL = 1024
CHUNK = 64
HEAD = 128
RET_HEADS = 4
DIFF_HEADS = 4
DIFF_DK = 64
SEG = 512
N_SEG = 7
D_IN = N_SEG * SEG
D_FF = 4 * D_MODEL
ROPE_THETA = 10000.0
NORM_EPS = 1e-6
SUBLN_EPS = 1e-5
NEG_INF = -1e30
ATTN_BLK = 256
RET_BLK = 256
RET_BLOCKS_PER_STEP = 2
Q_SCALE = DIFF_DK ** -0.5 * math.log2(math.e)
ONES_ROWS = 16
VAUG = HEAD + ONES_ROWS
SKEW = 2

VMEM_LIMIT_BYTES = 56 * 1024 * 1024

F32 = jnp.float32
BF16 = jnp.bfloat16

_NT = (((1,), (1,)), ((), ()))
_TN = (((0,), (0,)), ((), ()))


def _params(*sem):
    return pltpu.CompilerParams(dimension_semantics=sem, vmem_limit_bytes=VMEM_LIMIT_BYTES)


def _const_spec(shape):
    return pl.BlockSpec(shape, lambda *_: (0,) * len(shape), pipeline_mode=pl.Buffered(1))


def _rope_table_kernel(c128_ref, s128_ref, c64_ref, s64_ref, *, seq, pos0):
    n = c128_ref.shape[0]
    lane = lax.broadcasted_iota(jnp.int32, (n, HEAD), 1)
    pos = (lax.broadcasted_iota(jnp.int32, (n, HEAD), 0) % seq + pos0).astype(F32)
    log_theta = math.log(ROPE_THETA)

    def table(d):
        half = d // 2
        r = lane % d
        i = (r % half).astype(F32)
        inv = jnp.exp(i * (-2.0 * log_theta / d))
        ang = pos * inv
        sign = jnp.where(r < half, -1.0, 1.0).astype(F32)
        return jnp.cos(ang), jnp.sin(ang) * sign

    c128_ref[...], s128_ref[...] = table(HEAD)
    c64_ref[...], s64_ref[...] = table(DIFF_DK)


def _rope_tables(n_rows, seq, pos0):
    shp = jax.ShapeDtypeStruct((n_rows, HEAD), F32)
    return pl.pallas_call(
        functools.partial(_rope_table_kernel, seq=seq, pos0=pos0),
        out_shape=(shp,) * 4,
        name="rope_tables",
    )()


def _proj_kernel(x_ref, nw_ref, w_ref, c128_ref, s128_ref, c64_ref, s64_ref,
                 rq_ref, rk_ref, rv_ref, rg_ref, dq_ref, dk_ref, dv_ref, dkb_ref, dvb_ref, *, transposed):
    tm = x_ref.shape[0]
    half_rows = tm // 2

    def norm_rows(rows):
        x = x_ref[rows, :]
        ms = jnp.mean(x * x, axis=-1, keepdims=True)
        return (x * lax.rsqrt(ms + NORM_EPS) * nw_ref[...]).astype(BF16)

    h_lo, h_hi = norm_rows(slice(0, half_rows)), norm_rows(slice(half_rows, tm))
    h = jnp.concatenate([h_lo, h_hi], axis=0)

    def seg(s, split=False):
        w = w_ref[:, s * SEG:(s + 1) * SEG]
        if split:
            return jnp.concatenate([jnp.dot(h_lo, w, preferred_element_type=F32),
                                    jnp.dot(h_hi, w, preferred_element_type=F32)], axis=0)
        return jnp.dot(h, w, preferred_element_type=F32)

    c128, s128 = c128_ref[...], s128_ref[...]
    c64, s64 = c64_ref[...], s64_ref[...]
    lane = lax.broadcasted_iota(jnp.int32, (tm, HEAD), 1)
    first_half64 = (lane % DIFF_DK) < (DIFF_DK // 2)

    def rope128(z):
        return z * c128 + pltpu.roll(z, HEAD // 2, 1) * s128

    def rope64(z):
        rot = jnp.where(first_half64, pltpu.roll(z, HEAD - DIFF_DK // 2, 1),
                        pltpu.roll(z, DIFF_DK // 2, 1))
        return z * c64 + rot * s64

    def store_blocks_t(ref, sl, zt, width):
        for blk in range(tm // width):
            ref[blk, sl, :] = zt[:, blk * width:(blk + 1) * width].astype(BF16)

    z = seg(6, split=True)
    for hd in range(DIFF_HEADS):
        dv_ref[pl.ds(hd, tm, stride=DIFF_HEADS), :] = z[:, hd * HEAD:(hd + 1) * HEAD]
    if transposed:
        for hd in range(SEG // HEAD):
            store_blocks_t(dvb_ref, slice(hd * VAUG, hd * VAUG + HEAD), z[:, hd * HEAD:(hd + 1) * HEAD].T,
                           ATTN_BLK)
            dvb_ref[:, hd * VAUG + HEAD:(hd + 1) * VAUG, :] = jnp.ones(
                (tm // ATTN_BLK, ONES_ROWS, ATTN_BLK), BF16)
    else:
        dvb_ref[...] = z.astype(BF16)
    z = seg(5)
    for hd in range(SEG // HEAD):
        sl = slice(hd * HEAD, (hd + 1) * HEAD)
        kr = rope64(z[:, sl])
        dkb_ref[:, sl] = kr.astype(BF16)
        if transposed:
            dk_ref[0, sl, :] = kr.T
        else:
            dk_ref[:, sl] = kr
    z = seg(4)
    for hd in range(SEG // HEAD):
        sl = slice(hd * HEAD, (hd + 1) * HEAD)
        qr = rope64(z[:, sl]) * Q_SCALE
        if transposed:
            store_blocks_t(dq_ref, sl, qr.T, ATTN_BLK)
        else:
            dq_ref[:, sl] = qr.astype(BF16)
    z = seg(1)
    for hd in range(SEG // HEAD):
        sl = slice(hd * HEAD, (hd + 1) * HEAD)
        kr = rope128(z[:, sl]) * (HEAD ** -0.5)
        if transposed:
            store_blocks_t(rk_ref, sl, kr.T, RET_BLK)
        else:
            rk_ref[:, sl] = kr.astype(BF16)
    z = seg(0)
    for hd in range(SEG // HEAD):
        sl = slice(hd * HEAD, (hd + 1) * HEAD)
        rq_ref[:, sl] = rope128(z[:, sl]).astype(BF16)
    rv_ref[...] = seg(2).astype(BF16)
    rg_ref[...] = seg(3).astype(BF16)


def _project(x, norm_w, w_in, tables, tm, seq, transposed):
    t = x.shape[0]
    nt = tables[0].shape[0] // tm
    row = pl.BlockSpec((tm, D_MODEL), lambda i: (i, 0))
    out = pl.BlockSpec((tm, SEG), lambda i: (i, 0))
    tab = pl.BlockSpec((tm, HEAD), lambda i: (i % nt, 0))
    b16 = jax.ShapeDtypeStruct((t, SEG), BF16)
    f32 = jax.ShapeDtypeStruct((t, SEG), F32)
    dv_spec = pl.BlockSpec((tm * DIFF_HEADS, HEAD), lambda i: (i, 0))
    dv_shape = jax.ShapeDtypeStruct((t * DIFF_HEADS, HEAD), F32)
    if transposed:
        per_seq = seq // tm
        def blk_t(rows, width):
            return (pl.BlockSpec((tm // width, rows, width), lambda i: (i, 0, 0)),
                    jax.ShapeDtypeStruct((t // width, rows, width), BF16))
        (rk_spec, rk_shape), (q_spec, q_shape) = blk_t(SEG, RET_BLK), blk_t(SEG, ATTN_BLK)
        v_spec, v_shape = blk_t(DIFF_HEADS * VAUG, ATTN_BLK)
        dk_t = pl.BlockSpec((1, SEG, tm), lambda i: (i // per_seq, 0, i % per_seq))
        out_specs = [out, rk_spec, out, out, q_spec, dk_t, dv_spec, out, v_spec]
        out_shape = [b16, rk_shape, b16, b16, q_shape, jax.ShapeDtypeStruct((t // seq, SEG, seq), F32),
                     dv_shape, b16, v_shape]
    else:
        out_specs = [out] * 6 + [dv_spec, out, out]
        out_shape = [b16, b16, b16, b16, b16, f32, dv_shape, b16, b16]
    return pl.pallas_call(
        functools.partial(_proj_kernel, transposed=transposed),
        grid=(t // tm,),
        in_specs=[row, _const_spec((1, D_MODEL)), _const_spec((D_MODEL, D_IN)), tab, tab, tab, tab],
        out_specs=out_specs,
        out_shape=out_shape,
        compiler_params=_params("parallel"),
        name="in_proj",
    )(x, norm_w, w_in, *tables)


def _log_gamma(h):
    return math.log1p(-(2.0 ** (-5.0 - h)))


def _ret_kernel(*refs, blk, has_init, k_feature_major):
    if has_init:
        q_ref, k_ref, v_ref, g_ref, s0_ref, o_ref, sout_ref, state, dmat, qdec, kdec = refs
    else:
        q_ref, k_ref, v_ref, g_ref, o_ref, sout_ref, state, dmat, qdec, kdec = refs
    b, c = pl.program_id(0), pl.program_id(1)

    @pl.when((b == 0) & (c == 0))
    def _():
        rel = (lax.broadcasted_iota(jnp.int32, (blk, blk), 0)
               - lax.broadcasted_iota(jnp.int32, (blk, blk), 1))
        relf = jnp.maximum(rel, 0).astype(F32)
        row = lax.broadcasted_iota(jnp.int32, qdec.shape[1:], 0).astype(F32)
        along = lax.broadcasted_iota(jnp.int32, kdec.shape[1:], 1 if k_feature_major else 0).astype(F32)
        for h in range(RET_HEADS):
            lg = _log_gamma(h)
            dmat[h] = jnp.where(rel >= 0, jnp.exp(lg * relf), 0.0)
            qdec[h] = jnp.exp(lg * (row + 1.0))
            kdec[h] = jnp.exp(lg * (blk - 1.0 - along))

    @pl.when(c == 0)
    def _():
        if has_init:
            state[...] = s0_ref[0]
        else:
            state[...] = jnp.zeros_like(state)

    for sub in range(q_ref.shape[0] // blk):
        rows = slice(sub * blk, (sub + 1) * blk)
        for h in range(RET_HEADS):
            sl = slice(h * HEAD, (h + 1) * HEAD)
            q, v = q_ref[rows, sl], v_ref[rows, sl]
            st = state[h]
            if k_feature_major:
                kt = k_ref[sub, sl, :]
                s = jnp.dot(q, kt, preferred_element_type=F32)
                kd = (kt.astype(F32) * kdec[h]).astype(BF16)
                upd = jnp.dot(kd, v, preferred_element_type=F32)
            else:
                k = k_ref[rows, sl]
                s = lax.dot_general(q, k, _NT, preferred_element_type=F32)
                kd = (k.astype(F32) * kdec[h]).astype(BF16)
                upd = lax.dot_general(kd, v, _TN, preferred_element_type=F32)
            o = jnp.dot((s * dmat[h]).astype(BF16), v, preferred_element_type=F32)
            o = o + jnp.dot(q, st.astype(BF16), preferred_element_type=F32) * qdec[h]
            state[h] = math.exp(_log_gamma(h) * blk) * st + upd
            ro = o * lax.rsqrt(jnp.mean(o * o, axis=-1, keepdims=True) + NORM_EPS)
            g = g_ref[rows, sl].astype(F32)
            o_ref[rows, sl] = (g * (1.0 / (1.0 + jnp.exp(-g))) * ro).astype(BF16)

    @pl.when(c == pl.num_programs(1) - 1)
    def _():
        sout_ref[0] = state[...]


def _retention(rq, rk, rv, rg, state0, batch, seq, blk, blocks_per_step):
    rows = blk * blocks_per_step
    nc = seq // rows
    tok = pl.BlockSpec((rows, SEG), lambda b, c: (b * nc + c, 0))
    st_spec = pl.BlockSpec((1, RET_HEADS, HEAD, HEAD), lambda b, c: (b, 0, 0, 0))
    has_init = state0 is not None
    k_feature_major = rk.ndim == 3
    k_spec = (pl.BlockSpec((blocks_per_step, SEG, blk), lambda b, c: (b * nc + c, 0, 0))
              if k_feature_major else tok)
    args = (rq, rk, rv, rg) + ((state0,) if has_init else ())
    return pl.pallas_call(
        functools.partial(_ret_kernel, blk=blk, has_init=has_init, k_feature_major=k_feature_major),
        grid=(batch, nc),
        in_specs=[tok, k_spec, tok, tok] + ([st_spec] if has_init else []),
        out_specs=[tok, st_spec],
        out_shape=[jax.ShapeDtypeStruct((batch * seq, SEG), BF16),
                   jax.ShapeDtypeStruct((batch, RET_HEADS, HEAD, HEAD), F32)],
        scratch_shapes=[pltpu.VMEM((RET_HEADS, HEAD, HEAD), F32),
                        pltpu.VMEM((RET_HEADS, blk, blk), F32),
                        pltpu.VMEM((RET_HEADS, blk, HEAD), F32),
                        pltpu.VMEM((RET_HEADS, 1, blk) if k_feature_major else (RET_HEADS, blk, HEAD), F32)],
        compiler_params=_params("arbitrary", "arbitrary"),
        name="retention",
    )(*args)


def _lambda(lq1_ref, lk1_ref, lq2_ref, lk2_ref, lambda_init):
    a = jnp.sum(lq1_ref[...] * lk1_ref[...], axis=-1, keepdims=True)
    b = jnp.sum(lq2_ref[...] * lk2_ref[...], axis=-1, keepdims=True)
    return jnp.exp(a) - jnp.exp(b) + lambda_init


def _stack_heads(q):
    lane = lax.broadcasted_iota(jnp.int32, q.shape, 1)
    zero = jnp.zeros_like(q)
    return jnp.concatenate([jnp.where(lane < DIFF_DK, q, zero),
                            jnp.where(lane >= DIFF_DK, q, zero)], axis=0)


def _diff_finish(acc, l, n, lam, sub, lambda_init):
    o = acc[:n] * (1.0 / l[:n]) - lam * (acc[n:] * (1.0 / l[n:]))
    o = o * lax.rsqrt(jnp.mean(o * o, axis=-1, keepdims=True) + SUBLN_EPS)
    return o * sub * (1.0 - lambda_init)


def _attn_kernel(lq1_ref, lk1_ref, lq2_ref, lk2_ref, sub_ref, qt_ref, k_ref, vt_ref, o_ref,
                 qs_ref, m_ref, acc_ref, *, lambda_init):
    blk = ATTN_BLK
    qi = pl.program_id(1)
    row = lax.broadcasted_iota(jnp.int32, (HEAD, blk), 0)
    for h in range(DIFF_HEADS):
        qt = qt_ref[0, h * HEAD:(h + 1) * HEAD, :]
        zero = jnp.zeros_like(qt)
        qs_ref[h, :, :blk] = jnp.where(row < DIFF_DK, qt, zero)
        qs_ref[h, :, blk:] = jnp.where(row >= DIFF_DK, qt, zero)
    m_ref[...] = jnp.full_like(m_ref, NEG_INF)
    acc_ref[...] = jnp.zeros_like(acc_ref)

    def run(blocks):
        items = [(j, masked, h) for j, masked in blocks for h in range(DIFF_HEADS)]
        m_cur = [m_ref[h] for h in range(DIFF_HEADS)]
        if any(masked for _, masked in blocks):
            key = lax.broadcasted_iota(jnp.int32, (blk, 2 * blk), 0)
            qry = lax.broadcasted_iota(jnp.int32, (blk, 2 * blk), 1) % blk
            visible = key // CHUNK <= qry // CHUNK

        def scores(j, masked, h):
            start = pl.multiple_of(j * blk, blk)
            s = jnp.dot(k_ref[pl.ds(start, blk), h * HEAD:(h + 1) * HEAD], qs_ref[h],
                        preferred_element_type=F32)
            if masked:
                s = jnp.where(visible, s, NEG_INF)
            m_prev = m_cur[h]
            m_cur[h] = jnp.maximum(m_prev, jnp.max(s, axis=0, keepdims=True))
            return j, h, s, m_prev, m_cur[h]

        def accumulate(j, h, s, m_prev, m_new):
            alpha = jnp.exp2(m_prev - m_new)
            p = jnp.exp2(s - m_new).astype(BF16)
            pv = jnp.dot(vt_ref[j, h * VAUG:(h + 1) * VAUG, :], p, preferred_element_type=F32)
            acc_ref[h] = alpha * acc_ref[h] + pv

        pending = [scores(*it) for it in items[:SKEW]]
        for i in range(len(items)):
            current = pending.pop(0)
            if i + SKEW < len(items):
                pending.append(scores(*items[i + SKEW]))
            accumulate(*current)
        for h in range(DIFF_HEADS):
            m_ref[h] = m_cur[h]

    def body(t, carry):
        run([(2 * t, False), (2 * t + 1, False)])
        return carry

    lax.fori_loop(0, qi // 2, body, 0)

    @pl.when(qi % 2 == 1)
    def _():
        run([(qi - 1, False), (qi, True)])

    @pl.when(qi % 2 == 0)
    def _():
        run([(qi, True)])

    lam = _lambda(lq1_ref, lk1_ref, lq2_ref, lk2_ref, lambda_init)
    for h in range(DIFF_HEADS):
        o = acc_ref[h, :HEAD, :] * (1.0 / acc_ref[h, HEAD:HEAD + 1, :])
        o = o[:, :blk] - lam * o[:, blk:]
        o = o * lax.rsqrt(jnp.mean(o * o, axis=0, keepdims=True) + SUBLN_EPS)
        o = o * sub_ref[...] * (1.0 - lambda_init)
        o_ref[:, h * HEAD:(h + 1) * HEAD] = o.T.astype(BF16)


def _diff_attention_prompt(lams, subln_col, dqt, dkb, dvt, batch, seq, lambda_init):
    blk = ATTN_BLK
    nq = seq // blk
    vec = _const_spec((1, DIFF_DK))
    return pl.pallas_call(
        functools.partial(_attn_kernel, lambda_init=lambda_init),
        grid=(batch, nq),
        in_specs=[vec] * 4 + [_const_spec((HEAD, 1)),
                  pl.BlockSpec((1, SEG, blk), lambda b, i: (b * nq + i, 0, 0)),
                  pl.BlockSpec((seq, SEG), lambda b, i: (b, 0)),
                  pl.BlockSpec((nq, DIFF_HEADS * VAUG, blk), lambda b, i: (b, 0, 0))],
        out_specs=pl.BlockSpec((blk, SEG), lambda b, i: (b * nq + i, 0)),
        out_shape=jax.ShapeDtypeStruct((batch * seq, SEG), BF16),
        scratch_shapes=[pltpu.VMEM((DIFF_HEADS, HEAD, 2 * blk), BF16),
                        pltpu.VMEM((DIFF_HEADS, 1, 2 * blk), F32),
                        pltpu.VMEM((DIFF_HEADS, VAUG, 2 * blk), F32)],
        compiler_params=_params("parallel", "arbitrary"),
        name="diff_attn_prompt",
    )(*lams, subln_col, dqt, dkb, dvt)


def _attn_sample_kernel(lq1_ref, lk1_ref, lq2_ref, lk2_ref, sub_ref, q_ref, kpt_ref, vp_ref,
                        kn_ref, vn_ref, o_ref, *, lambda_init):
    n = q_ref.shape[0]
    past = kpt_ref.shape[2]
    lam = _lambda(lq1_ref, lk1_ref, lq2_ref, lk2_ref, lambda_init)
    for h in range(DIFF_HEADS):
        sl = slice(h * HEAD, (h + 1) * HEAD)
        qs = _stack_heads(q_ref[:, sl])
        s_p = jnp.dot(qs, kpt_ref[0, sl, :].astype(BF16), preferred_element_type=F32)
        s_n = lax.dot_general(qs, kn_ref[:, sl], _NT, preferred_element_type=F32)
        m = jnp.maximum(jnp.max(s_p, axis=-1, keepdims=True), jnp.max(s_n, axis=-1, keepdims=True))
        p_p = jnp.exp2(s_p - m)
        p_n = jnp.exp2(s_n - m)
        l = jnp.sum(p_p, axis=-1, keepdims=True) + jnp.sum(p_n, axis=-1, keepdims=True)
        v_p = vp_ref[0, pl.ds(h, past, stride=DIFF_HEADS), :].astype(BF16)
        acc = (jnp.dot(p_p.astype(BF16), v_p, preferred_element_type=F32)
               + jnp.dot(p_n.astype(BF16), vn_ref[:, sl], preferred_element_type=F32))
        o_ref[:, sl] = _diff_finish(acc, l, n, lam, sub_ref[...], lambda_init).astype(BF16)


def _diff_attention_sample(lams, subln, dq, kt_past, v_past, dkb, dvb, batch, seq, lambda_init):
    past = kt_past.shape[2]
    vec = _const_spec((1, DIFF_DK))
    new = pl.BlockSpec((seq, SEG), lambda b: (b, 0))
    return pl.pallas_call(
        functools.partial(_attn_sample_kernel, lambda_init=lambda_init),
        grid=(batch,),
        in_specs=[vec] * 4 + [_const_spec((1, HEAD)), new,
                  pl.BlockSpec((1, SEG, past), lambda b: (b, 0, 0)),
                  pl.BlockSpec((1, DIFF_HEADS * past, HEAD), lambda b: (b, 0, 0)), new, new],
        out_specs=new,
        out_shape=jax.ShapeDtypeStruct((batch * seq, SEG), BF16),
        compiler_params=_params("parallel"),
        name="diff_attn_sample",
    )(*lams, subln, dq, kt_past, v_past, dkb, dvb)


def _rms(x, w):
    return x * lax.rsqrt(jnp.mean(x * x, axis=-1, keepdims=True) + NORM_EPS) * w


def _mlp_kernel(ret_ref, dif_ref, x_ref, wo_ref, n1_ref, n2_ref, n3_ref, wu_ref, wd_ref, y_ref, u_ref,
                *, ff_blk):
    tm = x_ref.shape[0]
    halves = (slice(0, tm // 2), slice(tm // 2, tm))

    def front(rows):
        mix = (jnp.dot(ret_ref[rows, :], wo_ref[:SEG, :], preferred_element_type=F32)
               + jnp.dot(dif_ref[rows, :], wo_ref[SEG:, :], preferred_element_type=F32))
        x1 = x_ref[rows, :] + _rms(mix, n1_ref[...])
        return x1, _rms(x1, n2_ref[...]).astype(BF16)

    def up(h, rows, c):
        sl = slice(c * ff_blk, (c + 1) * ff_blk)
        u = jnp.maximum(jnp.dot(h, wu_ref[:, sl], preferred_element_type=F32), 0.0)
        u_ref[rows, sl] = (u * u).astype(BF16)

    (x1_lo, h_lo), (x1_hi, h_hi) = front(halves[0]), front(halves[1])
    up(h_lo, halves[0], 0)
    up(h_hi, halves[1], 0)
    h = jnp.concatenate([h_lo, h_hi], axis=0)
    for c in range(1, D_FF // ff_blk):
        up(h, slice(None), c)
    for rows, x1 in zip(halves, (x1_lo, x1_hi)):
        f = jnp.dot(u_ref[rows, :], wd_ref[...], preferred_element_type=F32)
        y_ref[rows, :] = x1 + _rms(f, n3_ref[...])


def _out_mlp(ret_out, diff_out, x, w_out, n1, n2, n3, w_up, w_down, tm, ff_blk=1024):
    t = x.shape[0]
    half = pl.BlockSpec((tm, SEG), lambda i: (i, 0))
    row = pl.BlockSpec((tm, D_MODEL), lambda i: (i, 0))
    nrm = _const_spec((1, D_MODEL))
    return pl.pallas_call(
        functools.partial(_mlp_kernel, ff_blk=ff_blk),
        grid=(t // tm,),
        in_specs=[half, half, row, _const_spec((D_MODEL, D_MODEL)), nrm, nrm, nrm,
                  _const_spec((D_MODEL, D_FF)), _const_spec((D_FF, D_MODEL))],
        out_specs=row,
        out_shape=jax.ShapeDtypeStruct((t, D_MODEL), F32),
        scratch_shapes=[pltpu.VMEM((tm, D_FF), BF16)],
        compiler_params=_params("parallel"),
        name="out_mlp",
    )(ret_out, diff_out, x, w_out, n1, n2, n3, w_up, w_down)


def _layer(x, pos0, state0, k_past, v_past, w, lambda_init, tm, ret_blk):
    batch, seq, _ = x.shape
    prompt = k_past is None
    xf = x.reshape(batch * seq, D_MODEL)
    tables = _rope_tables(max(seq, tm), seq, pos0)
    rq, rk, rv, rg, dq, dk, dv, dkb, dvb = _project(xf, w["norm_mix_pre"], w["w_in"], tables, tm, seq,
                                                    transposed=prompt)
    ret_out, new_state = _retention(rq, rk, rv, rg, state0, batch, seq, ret_blk,
                                    blocks_per_step=RET_BLOCKS_PER_STEP if prompt else 1)
    lams = (w["lambda_q1"], w["lambda_k1"], w["lambda_q2"], w["lambda_k2"])
    if prompt:
        diff_out = _diff_attention_prompt(lams, w["diff_subln"].reshape(HEAD, 1), dq, dkb, dvb,
                                          batch, seq, lambda_init)
        k_new = dk.reshape(batch, 2 * DIFF_HEADS, DIFF_DK, seq).transpose(0, 3, 1, 2)
    else:
        past = k_past.shape[1]
        kt_past = k_past.transpose(0, 2, 3, 1).reshape(batch, SEG, past)
        diff_out = _diff_attention_sample(
            lams, w["diff_subln"], dq, kt_past, v_past.reshape(batch, past * DIFF_HEADS, HEAD),
            dkb, dvb, batch, seq, lambda_init)
        k_new = dk.reshape(batch, seq, 2 * DIFF_HEADS, DIFF_DK)
    y = _out_mlp(ret_out, diff_out, xf, w["w_out"], w["norm_mix_post"], w["norm_ffn_pre"],
                 w["norm_ffn_post"], w["w_up"], w["w_down"], tm)
    return (y.reshape(batch, seq, D_MODEL), k_new, dv.reshape(batch, seq, DIFF_HEADS, HEAD), new_state)


def kernel(x_prompt, x_sample, cache_diff_k, cache_diff_v, state_ret, norm_mix_pre, norm_mix_post, w_in, diff_subln, lambda_q1, lambda_k1, lambda_q2, lambda_k2, w_out, norm_ffn_pre, norm_ffn_post, w_up, w_down):
    depth = w_in.shape[0]
    past = cache_diff_k.shape[2]
    yp, ys = x_prompt, x_sample
    outs = [[] for _ in range(6)]
    for l in range(depth):
        lambda_init = 0.8 - 0.6 * math.exp(-0.3 * l)
        w = dict(
            norm_mix_pre=norm_mix_pre[l][None], norm_mix_post=norm_mix_post[l][None],
            w_in=w_in[l].astype(BF16), diff_subln=diff_subln[l][None],
            lambda_q1=lambda_q1[l][None], lambda_k1=lambda_k1[l][None],
            lambda_q2=lambda_q2[l][None], lambda_k2=lambda_k2[l][None],
            w_out=w_out[l].astype(BF16), norm_ffn_pre=norm_ffn_pre[l][None],
            norm_ffn_post=norm_ffn_post[l][None],
            w_up=w_up[l].astype(BF16), w_down=w_down[l].astype(BF16))
        yp, kp, vp, sp = _layer(yp, 0, None, None, None, w, lambda_init,
                                tm=512, ret_blk=RET_BLK)
        ys, ks, vs, ss = _layer(ys, past, state_ret[l], cache_diff_k[l], cache_diff_v[l], w, lambda_init,
                                tm=x_sample.shape[0] * x_sample.shape[1], ret_blk=x_sample.shape[1])
        for lst, val in zip(outs, (kp, vp, sp, ks, vs, ss)):
            lst.append(val)
    return (yp, ys) + tuple(jnp.stack(o) for o in outs)
```

```python
import functools
import math

import jax
import jax.numpy as jnp
from jax import lax
from jax.experimental import pallas as pl
from jax.experimental.pallas import tpu as pltpu

D_MODEL = 1024
CHUNK = 64
HEAD = 128
RET_HEADS = 4
DIFF_HEADS = 4
DIFF_DK = 64
SEG = 512
N_SEG = 7
D_IN = N_SEG * SEG
D_FF = 4 * D_MODEL
ROPE_THETA = 10000.0
NORM_EPS = 1e-6
SUBLN_EPS = 1e-5
NEG_INF = -1e30
TM_PROJ = 1024
TM_MLP = 512
ATTN_BLK = 256
RET_BLK = 256
RET_BLOCKS_PER_ITER = 2
Q_SCALE = DIFF_DK ** -0.5 * math.log2(math.e)
ONES_ROWS = 16
VAUG = HEAD + ONES_ROWS
SKEW = 2

VMEM_LIMIT_BYTES = 56 * 1024 * 1024

F32 = jnp.float32
BF16 = jnp.bfloat16

_NT = (((1,), (1,)), ((), ()))
_TN = (((0,), (0,)), ((), ()))


def _params(*sem):
    return pltpu.CompilerParams(dimension_semantics=sem, vmem_limit_bytes=VMEM_LIMIT_BYTES)


def _const_spec(shape):
    return pl.BlockSpec(shape, lambda *_: (0,) * len(shape), pipeline_mode=pl.Buffered(1))


def _rope_table_kernel(c128_ref, s128_ref, c64_ref, s64_ref, *, seq, pos0):
    n = c128_ref.shape[0]
    lane = lax.broadcasted_iota(jnp.int32, (n, HEAD), 1)
    pos = (lax.broadcasted_iota(jnp.int32, (n, HEAD), 0) % seq + pos0).astype(F32)
    log_theta = math.log(ROPE_THETA)

    def table(d):
        half = d // 2
        r = lane % d
        i = (r % half).astype(F32)
        inv = jnp.exp(i * (-2.0 * log_theta / d))
        ang = pos * inv
        sign = jnp.where(r < half, -1.0, 1.0).astype(F32)
        return jnp.cos(ang), jnp.sin(ang) * sign

    c128_ref[...], s128_ref[...] = table(HEAD)
    c64_ref[...], s64_ref[...] = table(DIFF_DK)


def _rope_tables(n_rows, seq, pos0):
    shp = jax.ShapeDtypeStruct((n_rows, HEAD), F32)
    return pl.pallas_call(
        functools.partial(_rope_table_kernel, seq=seq, pos0=pos0),
        out_shape=(shp,) * 4,
        name="rope_tables",
    )()


def _proj_kernel(x_ref, nw_ref, w_ref, c128_ref, s128_ref, c64_ref, s64_ref,
                 rq_ref, rk_ref, rv_ref, rg_ref, dq_ref, dk_ref, dv_ref, dkb_ref, dvb_ref, *, transposed):
    tm = x_ref.shape[0]
    half_rows = tm // 2

    def norm_rows(rows):
        x = x_ref[rows, :]
        ms = jnp.mean(x * x, axis=-1, keepdims=True)
        return (x * lax.rsqrt(ms + NORM_EPS) * nw_ref[...]).astype(BF16)

    h_lo, h_hi = norm_rows(slice(0, half_rows)), norm_rows(slice(half_rows, tm))
    h = jnp.concatenate([h_lo, h_hi], axis=0)

    def seg(s, split=False):
        w = w_ref[:, s * SEG:(s + 1) * SEG]
        if split:
            return jnp.concatenate([jnp.dot(h_lo, w, preferred_element_type=F32),
                                    jnp.dot(h_hi, w, preferred_element_type=F32)], axis=0)
        return jnp.dot(h, w, preferred_element_type=F32)

    c128, s128 = c128_ref[...], s128_ref[...]
    c64, s64 = c64_ref[...], s64_ref[...]
    lane = lax.broadcasted_iota(jnp.int32, (tm, HEAD), 1)
    first_half64 = (lane % DIFF_DK) < (DIFF_DK // 2)

    def rope128(z):
        return z * c128 + pltpu.roll(z, HEAD // 2, 1) * s128

    def rope64(z):
        rot = jnp.where(first_half64, pltpu.roll(z, HEAD - DIFF_DK // 2, 1),
                        pltpu.roll(z, DIFF_DK // 2, 1))
        return z * c64 + rot * s64

    def store_blocks_t(ref, sl, zt, width):
        for blk in range(tm // width):
            ref[blk, sl, :] = zt[:, blk * width:(blk + 1) * width].astype(BF16)

    z = seg(6, split=True)
    for hd in range(DIFF_HEADS):
        dv_ref[pl.ds(hd, tm, stride=DIFF_HEADS), :] = z[:, hd * HEAD:(hd + 1) * HEAD]
    if transposed:
        for hd in range(SEG // HEAD):
            store_blocks_t(dvb_ref, slice(hd * VAUG, hd * VAUG + HEAD), z[:, hd * HEAD:(hd + 1) * HEAD].T,
                           ATTN_BLK)
            dvb_ref[:, hd * VAUG + HEAD:(hd + 1) * VAUG, :] = jnp.ones(
                (tm // ATTN_BLK, ONES_ROWS, ATTN_BLK), BF16)
    else:
        dvb_ref[...] = z.astype(BF16)
    z = seg(5)
    for hd in range(SEG // HEAD):
        sl = slice(hd * HEAD, (hd + 1) * HEAD)
        kr = rope64(z[:, sl])
        dkb_ref[:, sl] = kr.astype(BF16)
        if transposed:
            dk_ref[0, sl, :] = kr.T
        else:
            dk_ref[:, sl] = kr
    z = seg(4)
    for hd in range(SEG // HEAD):
        sl = slice(hd * HEAD, (hd + 1) * HEAD)
        qr = rope64(z[:, sl]) * Q_SCALE
        if transposed:
            store_blocks_t(dq_ref, sl, qr.T, ATTN_BLK)
        else:
            dq_ref[:, sl] = qr.astype(BF16)
    z = seg(1)
    for hd in range(SEG // HEAD):
        sl = slice(hd * HEAD, (hd + 1) * HEAD)
        kr = rope128(z[:, sl]) * (HEAD ** -0.5)
        if transposed:
            store_blocks_t(rk_ref, sl, kr.T, RET_BLK)
        else:
            rk_ref[:, sl] = kr.astype(BF16)
    z = seg(0)
    for hd in range(SEG // HEAD):
        sl = slice(hd * HEAD, (hd + 1) * HEAD)
        rq_ref[:, sl] = rope128(z[:, sl]).astype(BF16)
    rv_ref[...] = seg(2).astype(BF16)
    rg_ref[...] = seg(3).astype(BF16)


def _project(x, norm_w, w_in, tables, tm, seq, transposed):
    t = x.shape[0]
    nt = tables[0].shape[0] // tm
    row = pl.BlockSpec((tm, D_MODEL), lambda i: (i, 0))
    out = pl.BlockSpec((tm, SEG), lambda i: (i, 0))
    tab = pl.BlockSpec((tm, HEAD), lambda i: (i % nt, 0))
    b16 = jax.ShapeDtypeStruct((t, SEG), BF16)
    f32 = jax.ShapeDtypeStruct((t, SEG), F32)
    dv_spec = pl.BlockSpec((tm * DIFF_HEADS, HEAD), lambda i: (i, 0))
    dv_shape = jax.ShapeDtypeStruct((t * DIFF_HEADS, HEAD), F32)
    if transposed:
        per_seq = seq // tm
        def blk_t(rows, width):
            return (pl.BlockSpec((tm // width, rows, width), lambda i: (i, 0, 0)),
                    jax.ShapeDtypeStruct((t // width, rows, width), BF16))
        (rk_spec, rk_shape), (q_spec, q_shape) = blk_t(SEG, RET_BLK), blk_t(SEG, ATTN_BLK)
        v_spec, v_shape = blk_t(DIFF_HEADS * VAUG, ATTN_BLK)
        dk_t = pl.BlockSpec((1, SEG, tm), lambda i: (i // per_seq, 0, i % per_seq))
        out_specs = [out, rk_spec, out, out, q_spec, dk_t, dv_spec, out, v_spec]
        out_shape = [b16, rk_shape, b16, b16, q_shape, jax.ShapeDtypeStruct((t // seq, SEG, seq), F32),
                     dv_shape, b16, v_shape]
    else:
        out_specs = [out] * 6 + [dv_spec, out, out]
        out_shape = [b16, b16, b16, b16, b16, f32, dv_shape, b16, b16]
    return pl.pallas_call(
        functools.partial(_proj_kernel, transposed=transposed),
        grid=(t // tm,),
        in_specs=[row, _const_spec((1, D_MODEL)), _const_spec((D_MODEL, D_IN)), tab, tab, tab, tab],
        out_specs=out_specs,
        out_shape=out_shape,
        compiler_params=_params("parallel"),
        name="in_proj",
    )(x, norm_w, w_in, *tables)


def _log_gamma(h):
    return math.log1p(-(2.0 ** (-5.0 - h)))


def _ret_kernel(*refs, blk, has_init, k_feature_major, blocks_per_iter):
    if has_init:
        q_ref, k_ref, v_ref, g_ref, s0_ref, o_ref, sout_ref, state, dmat, qdec, kdec = refs
    else:
        q_ref, k_ref, v_ref, g_ref, o_ref, sout_ref, state, dmat, qdec, kdec = refs

    @pl.when(pl.program_id(0) == 0)
    def _():
        rel = (lax.broadcasted_iota(jnp.int32, (blk, blk), 0)
               - lax.broadcasted_iota(jnp.int32, (blk, blk), 1))
        relf = jnp.maximum(rel, 0).astype(F32)
        row = lax.broadcasted_iota(jnp.int32, qdec.shape[1:], 0).astype(F32)
        along = lax.broadcasted_iota(jnp.int32, kdec.shape[1:], 1 if k_feature_major else 0).astype(F32)
        for h in range(RET_HEADS):
            lg = _log_gamma(h)
            dmat[h] = jnp.where(rel >= 0, jnp.exp(lg * relf), 0.0)
            qdec[h] = jnp.exp(lg * (row + 1.0))
            kdec[h] = jnp.exp(lg * (blk - 1.0 - along))

    if has_init:
        state[...] = s0_ref[0]
    else:
        state[...] = jnp.zeros_like(state)

    def block(idx):
        rows = pl.ds(pl.multiple_of(idx * blk, blk), blk)
        for h in range(RET_HEADS):
            sl = slice(h * HEAD, (h + 1) * HEAD)
            q, v = q_ref[rows, sl], v_ref[rows, sl]
            st = state[h]
            if k_feature_major:
                kt = k_ref[idx, sl, :]
                s = jnp.dot(q, kt, preferred_element_type=F32)
                kd = (kt.astype(F32) * kdec[h]).astype(BF16)
                upd = jnp.dot(kd, v, preferred_element_type=F32)
            else:
                k = k_ref[rows, sl]
                s = lax.dot_general(q, k, _NT, preferred_element_type=F32)
                kd = (k.astype(F32) * kdec[h]).astype(BF16)
                upd = lax.dot_general(kd, v, _TN, preferred_element_type=F32)
            o = jnp.dot((s * dmat[h]).astype(BF16), v, preferred_element_type=F32)
            o = o + jnp.dot(q, st.astype(BF16), preferred_element_type=F32) * qdec[h]
            state[h] = math.exp(_log_gamma(h) * blk) * st + upd
            ro = o * lax.rsqrt(jnp.mean(o * o, axis=-1, keepdims=True) + NORM_EPS)
            g = g_ref[rows, sl].astype(F32)
            o_ref[rows, sl] = (g * (1.0 / (1.0 + jnp.exp(-g))) * ro).astype(BF16)

    def group(g, carry):
        for sub in range(blocks_per_iter):
            block(g * blocks_per_iter + sub)
        return carry

    lax.fori_loop(0, q_ref.shape[0] // (blk * blocks_per_iter), group, 0)
    sout_ref[0] = state[...]


def _retention(rq, rk, rv, rg, state0, batch, seq, blk, blocks_per_iter):
    tok = pl.BlockSpec((seq, SEG), lambda b: (b, 0))
    st_spec = pl.BlockSpec((1, RET_HEADS, HEAD, HEAD), lambda b: (b, 0, 0, 0))
    has_init = state0 is not None
    k_feature_major = rk.ndim == 3
    k_spec = pl.BlockSpec((seq // blk, SEG, blk), lambda b: (b, 0, 0)) if k_feature_major else tok
    args = (rq, rk, rv, rg) + ((state0,) if has_init else ())
    return pl.pallas_call(
        functools.partial(_ret_kernel, blk=blk, has_init=has_init, k_feature_major=k_feature_major,
                          blocks_per_iter=blocks_per_iter),
        grid=(batch,),
        in_specs=[tok, k_spec, tok, tok] + ([st_spec] if has_init else []),
        out_specs=[tok, st_spec],
        out_shape=[jax.ShapeDtypeStruct((batch * seq, SEG), BF16),
                   jax.ShapeDtypeStruct((batch, RET_HEADS, HEAD, HEAD), F32)],
        scratch_shapes=[pltpu.VMEM((RET_HEADS, HEAD, HEAD), F32),
                        pltpu.VMEM((RET_HEADS, blk, blk), F32),
                        pltpu.VMEM((RET_HEADS, blk, HEAD), F32),
                        pltpu.VMEM((RET_HEADS, 1, blk) if k_feature_major else (RET_HEADS, blk, HEAD), F32)],
        compiler_params=_params("arbitrary"),
        name="retention",
    )(*args)


def _lambda(lq1_ref, lk1_ref, lq2_ref, lk2_ref, lambda_init):
    a = jnp.sum(lq1_ref[...] * lk1_ref[...], axis=-1, keepdims=True)
    b = jnp.sum(lq2_ref[...] * lk2_ref[...], axis=-1, keepdims=True)
    return jnp.exp(a) - jnp.exp(b) + lambda_init


def _stack_heads(q):
    lane = lax.broadcasted_iota(jnp.int32, q.shape, 1)
    zero = jnp.zeros_like(q)
    return jnp.concatenate([jnp.where(lane < DIFF_DK, q, zero),
                            jnp.where(lane >= DIFF_DK, q, zero)], axis=0)


def _diff_finish(acc, l, n, lam, sub, lambda_init):
    o = acc[:n] * (1.0 / l[:n]) - lam * (acc[n:] * (1.0 / l[n:]))
    o = o * lax.rsqrt(jnp.mean(o * o, axis=-1, keepdims=True) + SUBLN_EPS)
    return o * sub * (1.0 - lambda_init)


def _attn_kernel(lq1_ref, lk1_ref, lq2_ref, lk2_ref, sub_ref, qt_ref, k_ref, vt_ref, o_ref,
                 qs_ref, m_ref, acc_ref, *, lambda_init):
    lam = _lambda(lq1_ref, lk1_ref, lq2_ref, lk2_ref, lambda_init)
    out_gain = sub_ref[...] * (1.0 - lambda_init)

    def query_block(qi, carry):
        _attn_query_block(qi, lam, out_gain, qt_ref, k_ref, vt_ref, o_ref, qs_ref, m_ref, acc_ref)
        return carry

    lax.fori_loop(0, qt_ref.shape[0], query_block, 0)


def _attn_query_block(qi, lam, out_gain, qt_ref, k_ref, vt_ref, o_ref, qs_ref, m_ref, acc_ref):
    blk = ATTN_BLK
    row = lax.broadcasted_iota(jnp.int32, (HEAD, blk), 0)
    for h in range(DIFF_HEADS):
        qt = qt_ref[qi, h * HEAD:(h + 1) * HEAD, :]
        zero = jnp.zeros_like(qt)
        qs_ref[h, :, :blk] = jnp.where(row < DIFF_DK, qt, zero)
        qs_ref[h, :, blk:] = jnp.where(row >= DIFF_DK, qt, zero)
    m_ref[...] = jnp.full_like(m_ref, NEG_INF)
    acc_ref[...] = jnp.zeros_like(acc_ref)
    out_rows = pl.ds(pl.multiple_of(qi * blk, blk), blk)

    def finish(h):
        o = acc_ref[h, :HEAD, :] * (1.0 / acc_ref[h, HEAD:HEAD + 1, :])
        o = o[:, :blk] - lam * o[:, blk:]
        o = o * lax.rsqrt(jnp.mean(o * o, axis=0, keepdims=True) + SUBLN_EPS) * out_gain
        o_ref[out_rows, h * HEAD:(h + 1) * HEAD] = o.T.astype(BF16)

    def run(blocks, last=False):
        items = [(j, masked, h) for j, masked in blocks for h in range(DIFF_HEADS)]
        m_cur = [m_ref[h] for h in range(DIFF_HEADS)]
        if any(masked for _, masked in blocks):
            key = lax.broadcasted_iota(jnp.int32, (blk, 2 * blk), 0)
            qry = lax.broadcasted_iota(jnp.int32, (blk, 2 * blk), 1) % blk
            visible = key // CHUNK <= qry // CHUNK

        def scores(j, masked, h):
            start = pl.multiple_of(j * blk, blk)
            s = jnp.dot(k_ref[pl.ds(start, blk), h * HEAD:(h + 1) * HEAD], qs_ref[h],
                        preferred_element_type=F32)
            if masked:
                s = jnp.where(visible, s, NEG_INF)
            m_prev = m_cur[h]
            m_cur[h] = jnp.maximum(m_prev, jnp.max(s, axis=0, keepdims=True))
            return j, h, s, m_prev, m_cur[h]

        def accumulate(j, h, s, m_prev, m_new):
            alpha = jnp.exp2(m_prev - m_new)
            p = jnp.exp2(s - m_new).astype(BF16)
            pv = jnp.dot(vt_ref[j, h * VAUG:(h + 1) * VAUG, :], p, preferred_element_type=F32)
            acc_ref[h] = alpha * acc_ref[h] + pv

        pending = [scores(*it) for it in items[:SKEW]]
        for i in range(len(items)):
            current = pending.pop(0)
            if i + SKEW < len(items):
                pending.append(scores(*items[i + SKEW]))
            accumulate(*current)
            if last and i >= len(items) - DIFF_HEADS:
                finish(current[1])
        if not last:
            for h in range(DIFF_HEADS):
                m_ref[h] = m_cur[h]

    def body(t, carry):
        run([(2 * t, False), (2 * t + 1, False)])
        return carry

    lax.fori_loop(0, qi // 2, body, 0)

    @pl.when(qi % 2 == 1)
    def _():
        run([(qi - 1, False), (qi, True)], last=True)

    @pl.when(qi % 2 == 0)
    def _():
        run([(qi, True)], last=True)


def _diff_attention_prompt(lams, subln_col, dqt, dkb, dvt, batch, seq, lambda_init):
    blk = ATTN_BLK
    nq = seq // blk
    vec = _const_spec((1, DIFF_DK))
    return pl.pallas_call(
        functools.partial(_attn_kernel, lambda_init=lambda_init),
        grid=(batch,),
        in_specs=[vec] * 4 + [_const_spec((HEAD, 1)),
                  pl.BlockSpec((nq, SEG, blk), lambda b: (b, 0, 0)),
                  pl.BlockSpec((seq, SEG), lambda b: (b, 0)),
                  pl.BlockSpec((nq, DIFF_HEADS * VAUG, blk), lambda b: (b, 0, 0))],
        out_specs=pl.BlockSpec((seq, SEG), lambda b: (b, 0)),
        out_shape=jax.ShapeDtypeStruct((batch * seq, SEG), BF16),
        scratch_shapes=[pltpu.VMEM((DIFF_HEADS, HEAD, 2 * blk), BF16),
                        pltpu.VMEM((DIFF_HEADS, 1, 2 * blk), F32),
                        pltpu.VMEM((DIFF_HEADS, VAUG, 2 * blk), F32)],
        compiler_params=_params("parallel"),
        name="diff_attn_prompt",
    )(*lams, subln_col, dqt, dkb, dvt)


def _attn_sample_kernel(lq1_ref, lk1_ref, lq2_ref, lk2_ref, sub_ref, q_ref, kpt_ref, vp_ref,
                        kn_ref, vn_ref, o_ref, *, lambda_init):
    n = q_ref.shape[0]
    past = kpt_ref.shape[2]
    lam = _lambda(lq1_ref, lk1_ref, lq2_ref, lk2_ref, lambda_init)
    for h in range(DIFF_HEADS):
        sl = slice(h * HEAD, (h + 1) * HEAD)
        qs = _stack_heads(q_ref[:, sl])
        s_p = jnp.dot(qs, kpt_ref[0, sl, :].astype(BF16), preferred_element_type=F32)
        s_n = lax.dot_general(qs, kn_ref[:, sl], _NT, preferred_element_type=F32)
        m = jnp.maximum(jnp.max(s_p, axis=-1, keepdims=True), jnp.max(s_n, axis=-1, keepdims=True))
        p_p = jnp.exp2(s_p - m)
        p_n = jnp.exp2(s_n - m)
        l = jnp.sum(p_p, axis=-1, keepdims=True) + jnp.sum(p_n, axis=-1, keepdims=True)
        v_p = vp_ref[0, pl.ds(h, past, stride=DIFF_HEADS), :].astype(BF16)
        acc = (jnp.dot(p_p.astype(BF16), v_p, preferred_element_type=F32)
               + jnp.dot(p_n.astype(BF16), vn_ref[:, sl], preferred_element_type=F32))
        o_ref[:, sl] = _diff_finish(acc, l, n, lam, sub_ref[...], lambda_init).astype(BF16)


def _diff_attention_sample(lams, subln, dq, kt_past, v_past, dkb, dvb, batch, seq, lambda_init):
    past = kt_past.shape[2]
    vec = _const_spec((1, DIFF_DK))
    new = pl.BlockSpec((seq, SEG), lambda b: (b, 0))
    return pl.pallas_call(
        functools.partial(_attn_sample_kernel, lambda_init=lambda_init),
        grid=(batch,),
        in_specs=[vec] * 4 + [_const_spec((1, HEAD)), new,
                  pl.BlockSpec((1, SEG, past), lambda b: (b, 0, 0)),
                  pl.BlockSpec((1, DIFF_HEADS * past, HEAD), lambda b: (b, 0, 0)), new, new],
        out_specs=new,
        out_shape=jax.ShapeDtypeStruct((batch * seq, SEG), BF16),
        compiler_params=_params("parallel"),
        name="diff_attn_sample",
    )(*lams, subln, dq, kt_past, v_past, dkb, dvb)


def _rms(x, w):
    return x * lax.rsqrt(jnp.mean(x * x, axis=-1, keepdims=True) + NORM_EPS) * w


def _mlp_kernel(ret_ref, dif_ref, x_ref, wo_ref, n1_ref, n2_ref, n3_ref, wu_ref, wd_ref, y_ref, u_ref,
                *, ff_blk):
    tm = x_ref.shape[0]
    halves = (slice(0, tm // 2), slice(tm // 2, tm))

    def front(rows):
        mix = (jnp.dot(ret_ref[rows, :], wo_ref[:SEG, :], preferred_element_type=F32)
               + jnp.dot(dif_ref[rows, :], wo_ref[SEG:, :], preferred_element_type=F32))
        x1 = x_ref[rows, :] + _rms(mix, n1_ref[...])
        return x1, _rms(x1, n2_ref[...]).astype(BF16)

    def up(h, rows, c):
        sl = slice(c * ff_blk, (c + 1) * ff_blk)
        u = jnp.maximum(jnp.dot(h, wu_ref[:, sl], preferred_element_type=F32), 0.0)
        u_ref[rows, sl] = (u * u).astype(BF16)

    (x1_lo, h_lo), (x1_hi, h_hi) = front(halves[0]), front(halves[1])
    up(h_lo, halves[0], 0)
    up(h_hi, halves[1], 0)
    h = jnp.concatenate([h_lo, h_hi], axis=0)
    for c in range(1, D_FF // ff_blk):
        up(h, slice(None), c)
    for rows, x1 in zip(halves, (x1_lo, x1_hi)):
        f = jnp.dot(u_ref[rows, :], wd_ref[...], preferred_element_type=F32)
        y_ref[rows, :] = x1 + _rms(f, n3_ref[...])


def _out_mlp(ret_out, diff_out, x, w_out, n1, n2, n3, w_up, w_down, tm, ff_blk=1024):
    t = x.shape[0]
    half = pl.BlockSpec((tm, SEG), lambda i: (i, 0))
    row = pl.BlockSpec((tm, D_MODEL), lambda i: (i, 0))
    nrm = _const_spec((1, D_MODEL))
    return pl.pallas_call(
        functools.partial(_mlp_kernel, ff_blk=ff_blk),
        grid=(t // tm,),
        in_specs=[half, half, row, _const_spec((D_MODEL, D_MODEL)), nrm, nrm, nrm,
                  _const_spec((D_MODEL, D_FF)), _const_spec((D_FF, D_MODEL))],
        out_specs=row,
        out_shape=jax.ShapeDtypeStruct((t, D_MODEL), F32),
        scratch_shapes=[pltpu.VMEM((tm, D_FF), BF16)],
        compiler_params=_params("parallel"),
        name="out_mlp",
    )(ret_out, diff_out, x, w_out, n1, n2, n3, w_up, w_down)


def _layer(x, pos0, state0, k_past, v_past, w, lambda_init, tm_proj, tm, ret_blk):
    batch, seq, _ = x.shape
    prompt = k_past is None
    xf = x.reshape(batch * seq, D_MODEL)
    tables = _rope_tables(max(seq, tm_proj), seq, pos0)
    rq, rk, rv, rg, dq, dk, dv, dkb, dvb = _project(xf, w["norm_mix_pre"], w["w_in"], tables, tm_proj, seq,
                                                    transposed=prompt)
    ret_out, new_state = _retention(rq, rk, rv, rg, state0, batch, seq, ret_blk,
                                    blocks_per_iter=RET_BLOCKS_PER_ITER if prompt else 1)
    lams = (w["lambda_q1"], w["lambda_k1"], w["lambda_q2"], w["lambda_k2"])
    if prompt:
        diff_out = _diff_attention_prompt(lams, w["diff_subln"].reshape(HEAD, 1), dq, dkb, dvb,
                                          batch, seq, lambda_init)
        k_new = dk.reshape(batch, 2 * DIFF_HEADS, DIFF_DK, seq).transpose(0, 3, 1, 2)
    else:
        past = k_past.shape[1]
        kt_past = k_past.transpose(0, 2, 3, 1).reshape(batch, SEG, past)
        diff_out = _diff_attention_sample(
            lams, w["diff_subln"], dq, kt_past, v_past.reshape(batch, past * DIFF_HEADS, HEAD),
            dkb, dvb, batch, seq, lambda_init)
        k_new = dk.reshape(batch, seq, 2 * DIFF_HEADS, DIFF_DK)
    y = _out_mlp(ret_out, diff_out, xf, w["w_out"], w["norm_mix_post"], w["norm_ffn_pre"],
                 w["norm_ffn_post"], w["w_up"], w["w_down"], tm)
    return (y.reshape(batch, seq, D_MODEL), k_new, dv.reshape(batch, seq, DIFF_HEADS, HEAD), new_state)


def kernel(x_prompt, x_sample, cache_diff_k, cache_diff_v, state_ret, norm_mix_pre, norm_mix_post, w_in, diff_subln, lambda_q1, lambda_k1, lambda_q2, lambda_k2, w_out, norm_ffn_pre, norm_ffn_post, w_up, w_down):
    depth = w_in.shape[0]
    past = cache_diff_k.shape[2]
    yp, ys = x_prompt, x_sample
    outs = [[] for _ in range(6)]
    for l in range(depth):
        lambda_init = 0.8 - 0.6 * math.exp(-0.3 * l)
        w = dict(
            norm_mix_pre=norm_mix_pre[l][None], norm_mix_post=norm_mix_post[l][None],
            w_in=w_in[l].astype(BF16), diff_subln=diff_subln[l][None],
            lambda_q1=lambda_q1[l][None], lambda_k1=lambda_k1[l][None],
            lambda_q2=lambda_q2[l][None], lambda_k2=lambda_k2[l][None],
            w_out=w_out[l].astype(BF16), norm_ffn_pre=norm_ffn_pre[l][None],
            norm_ffn_post=norm_ffn_post[l][None],
            w_up=w_up[l].astype(BF16), w_down=w_down[l].astype(BF16))
        yp, kp, vp, sp = _layer(yp, 0, None, None, None, w, lambda_init,
                                tm_proj=TM_PROJ, tm=TM_MLP, ret_blk=RET_BLK)
        n_sample = x_sample.shape[0] * x_sample.shape[1]
        ys, ks, vs, ss = _layer(ys, past, state_ret[l], cache_diff_k[l], cache_diff_v[l], w, lambda_init,
                                tm_proj=n_sample, tm=n_sample, ret_blk=x_sample.shape[1])
        for lst, val in zip(outs, (kp, vp, sp, ks, vs, ss)):
            lst.append(val)
    return (yp, ys) + tuple(jnp.stack(o) for o in outs)
```

```python
import functools
import math

import jax
import jax.numpy as jnp
from jax import lax
from jax.experimental import pallas as pl
from jax.experimental.pallas import tpu as pltpu

D_MODEL = 1024
CHUNK = 64
HEAD = 128
RET_HEADS = 4
DIFF_HEADS = 4
DIFF_DK = 64
SEG = 512
N_SEG = 7
D_IN = N_SEG * SEG
D_FF = 4 * D_MODEL
ROPE_THETA = 10000.0
NORM_EPS = 1e-6
SUBLN_EPS = 1e-5
NEG_INF = -1e30
TM_PROJ = 1024
TM_MLP = 1024
ATTN_BLK = 256
RET_BLK = 256
RET_BLOCKS_PER_ITER = 2
Q_SCALE = DIFF_DK ** -0.5 * math.log2(math.e)
ONES_ROWS = 16
VAUG = HEAD + ONES_ROWS
REGION_BLOCKS = 4
SKEW = 2

VMEM_LIMIT_BYTES = 56 * 1024 * 1024

F32 = jnp.float32
BF16 = jnp.bfloat16

_NT = (((1,), (1,)), ((), ()))
_TN = (((0,), (0,)), ((), ()))


def _params(*sem):
    return pltpu.CompilerParams(dimension_semantics=sem, vmem_limit_bytes=VMEM_LIMIT_BYTES)


def _const_spec(shape):
    return pl.BlockSpec(shape, lambda *_: (0,) * len(shape), pipeline_mode=pl.Buffered(1))


def _rope_table_kernel(c128_ref, s128_ref, c64_ref, s64_ref, *, seq, pos0):
    n = c128_ref.shape[0]
    lane = lax.broadcasted_iota(jnp.int32, (n, HEAD), 1)
    pos = (lax.broadcasted_iota(jnp.int32, (n, HEAD), 0) % seq + pos0).astype(F32)
    log_theta = math.log(ROPE_THETA)

    def table(d):
        half = d // 2
        r = lane % d
        i = (r % half).astype(F32)
        inv = jnp.exp(i * (-2.0 * log_theta / d))
        ang = pos * inv
        sign = jnp.where(r < half, -1.0, 1.0).astype(F32)
        return jnp.cos(ang), jnp.sin(ang) * sign

    c128_ref[...], s128_ref[...] = table(HEAD)
    c64_ref[...], s64_ref[...] = table(DIFF_DK)


def _rope_tables(n_rows, seq, pos0):
    shp = jax.ShapeDtypeStruct((n_rows, HEAD), F32)
    return pl.pallas_call(
        functools.partial(_rope_table_kernel, seq=seq, pos0=pos0),
        out_shape=(shp,) * 4,
        name="rope_tables",
    )()


def _proj_kernel(x_ref, nw_ref, w_ref, c128_ref, s128_ref, c64_ref, s64_ref,
                 rq_ref, rk_ref, rv_ref, rg_ref, dq_ref, dk_ref, dv_ref, dkb_ref, dvb_ref, *, transposed):
    tm = x_ref.shape[0]
    half_rows = tm // 2

    def norm_rows(rows):
        x = x_ref[rows, :]
        ms = jnp.mean(x * x, axis=-1, keepdims=True)
        return (x * lax.rsqrt(ms + NORM_EPS) * nw_ref[...]).astype(BF16)

    h_lo, h_hi = norm_rows(slice(0, half_rows)), norm_rows(slice(half_rows, tm))
    h = jnp.concatenate([h_lo, h_hi], axis=0)

    def seg(s, split=False):
        w = w_ref[:, s * SEG:(s + 1) * SEG]
        if split:
            return jnp.concatenate([jnp.dot(h_lo, w, preferred_element_type=F32),
                                    jnp.dot(h_hi, w, preferred_element_type=F32)], axis=0)
        return jnp.dot(h, w, preferred_element_type=F32)

    c128, s128 = c128_ref[...], s128_ref[...]
    c64, s64 = c64_ref[...], s64_ref[...]
    lane = lax.broadcasted_iota(jnp.int32, (tm, HEAD), 1)
    first_half64 = (lane % DIFF_DK) < (DIFF_DK // 2)

    def rope128(z):
        return z * c128 + pltpu.roll(z, HEAD // 2, 1) * s128

    def rope64(z):
        rot = jnp.where(first_half64, pltpu.roll(z, HEAD - DIFF_DK // 2, 1),
                        pltpu.roll(z, DIFF_DK // 2, 1))
        return z * c64 + rot * s64

    def store_blocks_t(ref, sl, zt, width):
        for blk in range(tm // width):
            ref[blk, sl, :] = zt[:, blk * width:(blk + 1) * width].astype(BF16)

    z = seg(6, split=True)
    for hd in range(DIFF_HEADS):
        dv_ref[pl.ds(hd, tm, stride=DIFF_HEADS), :] = z[:, hd * HEAD:(hd + 1) * HEAD]
    if transposed:
        for hd in range(SEG // HEAD):
            store_blocks_t(dvb_ref, slice(hd * VAUG, hd * VAUG + HEAD), z[:, hd * HEAD:(hd + 1) * HEAD].T,
                           ATTN_BLK)
            dvb_ref[:, hd * VAUG + HEAD:(hd + 1) * VAUG, :] = jnp.ones(
                (tm // ATTN_BLK, ONES_ROWS, ATTN_BLK), BF16)
    else:
        dvb_ref[...] = z.astype(BF16)
    z = seg(5)
    for hd in range(SEG // HEAD):
        sl = slice(hd * HEAD, (hd + 1) * HEAD)
        kr = rope64(z[:, sl])
        dkb_ref[:, sl] = kr.astype(BF16)
        if transposed:
            dk_ref[0, sl, :] = kr.T
        else:
            dk_ref[:, sl] = kr
    z = seg(4)
    for hd in range(SEG // HEAD):
        sl = slice(hd * HEAD, (hd + 1) * HEAD)
        qr = rope64(z[:, sl]) * Q_SCALE
        if transposed:
            store_blocks_t(dq_ref, sl, qr.T, ATTN_BLK)
        else:
            dq_ref[:, sl] = qr.astype(BF16)
    z = seg(1)
    for hd in range(SEG // HEAD):
        sl = slice(hd * HEAD, (hd + 1) * HEAD)
        kr = rope128(z[:, sl]) * (HEAD ** -0.5)
        if transposed:
            store_blocks_t(rk_ref, sl, kr.T, RET_BLK)
        else:
            rk_ref[:, sl] = kr.astype(BF16)
    z = seg(0)
    for hd in range(SEG // HEAD):
        sl = slice(hd * HEAD, (hd + 1) * HEAD)
        rq_ref[:, sl] = rope128(z[:, sl]).astype(BF16)
    rv_ref[...] = seg(2).astype(BF16)
    rg_ref[...] = seg(3).astype(BF16)


def _project(x, norm_w, w_in, tables, tm, seq, transposed):
    t = x.shape[0]
    nt = tables[0].shape[0] // tm
    row = pl.BlockSpec((tm, D_MODEL), lambda i: (i, 0))
    out = pl.BlockSpec((tm, SEG), lambda i: (i, 0))
    tab = pl.BlockSpec((tm, HEAD), lambda i: (i % nt, 0))
    b16 = jax.ShapeDtypeStruct((t, SEG), BF16)
    f32 = jax.ShapeDtypeStruct((t, SEG), F32)
    dv_spec = pl.BlockSpec((tm * DIFF_HEADS, HEAD), lambda i: (i, 0))
    dv_shape = jax.ShapeDtypeStruct((t * DIFF_HEADS, HEAD), F32)
    if transposed:
        per_seq = seq // tm
        def blk_t(rows, width):
            return (pl.BlockSpec((tm // width, rows, width), lambda i: (i, 0, 0)),
                    jax.ShapeDtypeStruct((t // width, rows, width), BF16))
        (rk_spec, rk_shape), (q_spec, q_shape) = blk_t(SEG, RET_BLK), blk_t(SEG, ATTN_BLK)
        v_spec, v_shape = blk_t(DIFF_HEADS * VAUG, ATTN_BLK)
        dk_t = pl.BlockSpec((1, SEG, tm), lambda i: (i // per_seq, 0, i % per_seq))
        out_specs = [out, rk_spec, out, out, q_spec, dk_t, dv_spec, out, v_spec]
        out_shape = [b16, rk_shape, b16, b16, q_shape, jax.ShapeDtypeStruct((t // seq, SEG, seq), F32),
                     dv_shape, b16, v_shape]
    else:
        out_specs = [out] * 6 + [dv_spec, out, out]
        out_shape = [b16, b16, b16, b16, b16, f32, dv_shape, b16, b16]
    return pl.pallas_call(
        functools.partial(_proj_kernel, transposed=transposed),
        grid=(t // tm,),
        in_specs=[row, _const_spec((1, D_MODEL)), _const_spec((D_MODEL, D_IN)), tab, tab, tab, tab],
        out_specs=out_specs,
        out_shape=out_shape,
        compiler_params=_params("parallel"),
        name="in_proj",
    )(x, norm_w, w_in, *tables)


def _log_gamma(h):
    return math.log1p(-(2.0 ** (-5.0 - h)))


def _ret_tables(dmat, qdec, kdec, blk, k_feature_major):
    rel = (lax.broadcasted_iota(jnp.int32, (blk, blk), 0)
           - lax.broadcasted_iota(jnp.int32, (blk, blk), 1))
    relf = jnp.maximum(rel, 0).astype(F32)
    row = lax.broadcasted_iota(jnp.int32, qdec.shape[1:], 0).astype(F32)
    along = lax.broadcasted_iota(jnp.int32, kdec.shape[1:], 1 if k_feature_major else 0).astype(F32)
    for h in range(RET_HEADS):
        lg = _log_gamma(h)
        dmat[h] = jnp.where(rel >= 0, jnp.exp(lg * relf), 0.0)
        qdec[h] = jnp.exp(lg * (row + 1.0))
        kdec[h] = jnp.exp(lg * (blk - 1.0 - along))


def _ret_head(idx, h, q_ref, k_ref, v_ref, g_ref, o_ref, state, dmat, qdec, kdec, blk, k_feature_major):
    rows = pl.ds(pl.multiple_of(idx * blk, blk), blk)
    sl = slice(h * HEAD, (h + 1) * HEAD)
    q, v = q_ref[rows, sl], v_ref[rows, sl]
    st = state[h]
    if k_feature_major:
        kt = k_ref[idx, sl, :]
        s = jnp.dot(q, kt, preferred_element_type=F32)
        kd = (kt.astype(F32) * kdec[h]).astype(BF16)
        upd = jnp.dot(kd, v, preferred_element_type=F32)
    else:
        k = k_ref[rows, sl]
        s = lax.dot_general(q, k, _NT, preferred_element_type=F32)
        kd = (k.astype(F32) * kdec[h]).astype(BF16)
        upd = lax.dot_general(kd, v, _TN, preferred_element_type=F32)
    o = jnp.dot((s * dmat[h]).astype(BF16), v, preferred_element_type=F32)
    o = o + jnp.dot(q, st.astype(BF16), preferred_element_type=F32) * qdec[h]
    state[h] = math.exp(_log_gamma(h) * blk) * st + upd
    ro = o * lax.rsqrt(jnp.mean(o * o, axis=-1, keepdims=True) + NORM_EPS)
    g = g_ref[rows, sl].astype(F32)
    o_ref[rows, sl] = (g * (1.0 / (1.0 + jnp.exp(-g))) * ro).astype(BF16)


def _ret_kernel(*refs, blk, has_init, k_feature_major, blocks_per_iter):
    if has_init:
        q_ref, k_ref, v_ref, g_ref, s0_ref, o_ref, sout_ref, state, dmat, qdec, kdec = refs
    else:
        q_ref, k_ref, v_ref, g_ref, o_ref, sout_ref, state, dmat, qdec, kdec = refs

    @pl.when(pl.program_id(0) == 0)
    def _():
        _ret_tables(dmat, qdec, kdec, blk, k_feature_major)

    if has_init:
        state[...] = s0_ref[0]
    else:
        state[...] = jnp.zeros_like(state)

    def group(g, carry):
        for sub in range(blocks_per_iter):
            for h in range(RET_HEADS):
                _ret_head(g * blocks_per_iter + sub, h, q_ref, k_ref, v_ref, g_ref, o_ref,
                          state, dmat, qdec, kdec, blk, k_feature_major)
        return carry

    lax.fori_loop(0, q_ref.shape[0] // (blk * blocks_per_iter), group, 0)
    sout_ref[0] = state[...]


def _retention(rq, rk, rv, rg, state0, batch, seq, blk, blocks_per_iter):
    tok = pl.BlockSpec((seq, SEG), lambda b: (b, 0))
    st_spec = pl.BlockSpec((1, RET_HEADS, HEAD, HEAD), lambda b: (b, 0, 0, 0))
    has_init = state0 is not None
    k_feature_major = rk.ndim == 3
    k_spec = pl.BlockSpec((seq // blk, SEG, blk), lambda b: (b, 0, 0)) if k_feature_major else tok
    args = (rq, rk, rv, rg) + ((state0,) if has_init else ())
    return pl.pallas_call(
        functools.partial(_ret_kernel, blk=blk, has_init=has_init, k_feature_major=k_feature_major,
                          blocks_per_iter=blocks_per_iter),
        grid=(batch,),
        in_specs=[tok, k_spec, tok, tok] + ([st_spec] if has_init else []),
        out_specs=[tok, st_spec],
        out_shape=[jax.ShapeDtypeStruct((batch * seq, SEG), BF16),
                   jax.ShapeDtypeStruct((batch, RET_HEADS, HEAD, HEAD), F32)],
        scratch_shapes=[pltpu.VMEM((RET_HEADS, HEAD, HEAD), F32),
                        pltpu.VMEM((RET_HEADS, blk, blk), F32),
                        pltpu.VMEM((RET_HEADS, blk, HEAD), F32),
                        pltpu.VMEM((RET_HEADS, 1, blk) if k_feature_major else (RET_HEADS, blk, HEAD), F32)],
        compiler_params=_params("arbitrary"),
        name="retention",
    )(*args)


def _lambda(lq1_ref, lk1_ref, lq2_ref, lk2_ref, lambda_init):
    a = jnp.sum(lq1_ref[...] * lk1_ref[...], axis=-1, keepdims=True)
    b = jnp.sum(lq2_ref[...] * lk2_ref[...], axis=-1, keepdims=True)
    return jnp.exp(a) - jnp.exp(b) + lambda_init


def _stack_heads(q):
    lane = lax.broadcasted_iota(jnp.int32, q.shape, 1)
    zero = jnp.zeros_like(q)
    return jnp.concatenate([jnp.where(lane < DIFF_DK, q, zero),
                            jnp.where(lane >= DIFF_DK, q, zero)], axis=0)


def _diff_finish(acc, l, n, lam, sub, lambda_init):
    o = acc[:n] * (1.0 / l[:n]) - lam * (acc[n:] * (1.0 / l[n:]))
    o = o * lax.rsqrt(jnp.mean(o * o, axis=-1, keepdims=True) + SUBLN_EPS)
    return o * sub * (1.0 - lambda_init)


def _attn_kernel(lq1_ref, lk1_ref, lq2_ref, lk2_ref, sub_ref, qt_ref, k_ref, vt_ref, o_ref,
                 qs_ref, m_ref, acc_ref, *, lambda_init):
    lam = _lambda(lq1_ref, lk1_ref, lq2_ref, lk2_ref, lambda_init)
    out_gain = sub_ref[...] * (1.0 - lambda_init)

    def query_block(qi, carry):
        _attn_query_block(qi, lam, out_gain, qt_ref, k_ref, vt_ref, o_ref, qs_ref, m_ref, acc_ref)
        return carry

    lax.fori_loop(0, qt_ref.shape[0], query_block, 0)


def _attn_query_block(qi, lam, out_gain, qt_ref, k_ref, vt_ref, o_ref, qs_ref, m_ref, acc_ref):
    blk = ATTN_BLK
    row = lax.broadcasted_iota(jnp.int32, (HEAD, blk), 0)
    for h in range(DIFF_HEADS):
        qt = qt_ref[qi, h * HEAD:(h + 1) * HEAD, :]
        zero = jnp.zeros_like(qt)
        qs_ref[h, :, :blk] = jnp.where(row < DIFF_DK, qt, zero)
        qs_ref[h, :, blk:] = jnp.where(row >= DIFF_DK, qt, zero)
    m_ref[...] = jnp.full_like(m_ref, NEG_INF)
    acc_ref[...] = jnp.zeros_like(acc_ref)
    out_rows = pl.ds(pl.multiple_of(qi * blk, blk), blk)

    def finish(h):
        o = acc_ref[h, :HEAD, :] * (1.0 / acc_ref[h, HEAD:HEAD + 1, :])
        o = o[:, :blk] - lam * o[:, blk:]
        o = o * lax.rsqrt(jnp.mean(o * o, axis=0, keepdims=True) + SUBLN_EPS) * out_gain
        o_ref[out_rows, h * HEAD:(h + 1) * HEAD] = o.T.astype(BF16)

    def run(blocks, last=False):
        items = [(j, masked, h) for j, masked in blocks for h in range(DIFF_HEADS)]
        m_cur = [m_ref[h] for h in range(DIFF_HEADS)]
        if any(masked for _, masked in blocks):
            key = lax.broadcasted_iota(jnp.int32, (blk, 2 * blk), 0)
            qry = lax.broadcasted_iota(jnp.int32, (blk, 2 * blk), 1) % blk
            visible = key // CHUNK <= qry // CHUNK

        def scores(j, masked, h):
            start = pl.multiple_of(j * blk, blk)
            s = jnp.dot(k_ref[pl.ds(start, blk), h * HEAD:(h + 1) * HEAD], qs_ref[h],
                        preferred_element_type=F32)
            if masked:
                s = jnp.where(visible, s, NEG_INF)
            m_prev = m_cur[h]
            m_cur[h] = jnp.maximum(m_prev, jnp.max(s, axis=0, keepdims=True))
            return j, h, s, m_prev, m_cur[h]

        def accumulate(j, h, s, m_prev, m_new):
            alpha = jnp.exp2(m_prev - m_new)
            p = jnp.exp2(s - m_new).astype(BF16)
            pv = jnp.dot(vt_ref[j, h * VAUG:(h + 1) * VAUG, :], p, preferred_element_type=F32)
            acc_ref[h] = alpha * acc_ref[h] + pv

        pending = [scores(*it) for it in items[:SKEW]]
        for i in range(len(items)):
            current = pending.pop(0)
            if i + SKEW < len(items):
                pending.append(scores(*items[i + SKEW]))
            accumulate(*current)
            if last and i >= len(items) - DIFF_HEADS:
                finish(current[1])
        if not last:
            for h in range(DIFF_HEADS):
                m_ref[h] = m_cur[h]

    def body(t, carry):
        run([(REGION_BLOCKS * t + i, False) for i in range(REGION_BLOCKS)])
        return carry

    lax.fori_loop(0, qi // REGION_BLOCKS, body, 0)

    for rest in range(REGION_BLOCKS):
        @pl.when(qi % REGION_BLOCKS == rest)
        def _(rest=rest):
            run([(qi - rest + i, False) for i in range(rest)] + [(qi, True)], last=True)


def _diff_attention_prompt(lams, subln_col, dqt, dkb, dvt, batch, seq, lambda_init):
    blk = ATTN_BLK
    nq = seq // blk
    vec = _const_spec((1, DIFF_DK))
    return pl.pallas_call(
        functools.partial(_attn_kernel, lambda_init=lambda_init),
        grid=(batch,),
        in_specs=[vec] * 4 + [_const_spec((HEAD, 1)),
                  pl.BlockSpec((nq, SEG, blk), lambda b: (b, 0, 0)),
                  pl.BlockSpec((seq, SEG), lambda b: (b, 0)),
                  pl.BlockSpec((nq, DIFF_HEADS * VAUG, blk), lambda b: (b, 0, 0))],
        out_specs=pl.BlockSpec((seq, SEG), lambda b: (b, 0)),
        out_shape=jax.ShapeDtypeStruct((batch * seq, SEG), BF16),
        scratch_shapes=[pltpu.VMEM((DIFF_HEADS, HEAD, 2 * blk), BF16),
                        pltpu.VMEM((DIFF_HEADS, 1, 2 * blk), F32),
                        pltpu.VMEM((DIFF_HEADS, VAUG, 2 * blk), F32)],
        compiler_params=_params("parallel"),
        name="diff_attn_prompt",
    )(*lams, subln_col, dqt, dkb, dvt)


def _attn_sample_kernel(lq1_ref, lk1_ref, lq2_ref, lk2_ref, sub_ref, q_ref, kpt_ref, vp_ref,
                        kn_ref, vn_ref, o_ref, *, lambda_init):
    n = q_ref.shape[0]
    past = kpt_ref.shape[2]
    lam = _lambda(lq1_ref, lk1_ref, lq2_ref, lk2_ref, lambda_init)
    for h in range(DIFF_HEADS):
        sl = slice(h * HEAD, (h + 1) * HEAD)
        qs = _stack_heads(q_ref[:, sl])
        s_p = jnp.dot(qs, kpt_ref[0, sl, :].astype(BF16), preferred_element_type=F32)
        s_n = lax.dot_general(qs, kn_ref[:, sl], _NT, preferred_element_type=F32)
        m = jnp.maximum(jnp.max(s_p, axis=-1, keepdims=True), jnp.max(s_n, axis=-1, keepdims=True))
        p_p = jnp.exp2(s_p - m)
        p_n = jnp.exp2(s_n - m)
        l = jnp.sum(p_p, axis=-1, keepdims=True) + jnp.sum(p_n, axis=-1, keepdims=True)
        v_p = vp_ref[0, pl.ds(h, past, stride=DIFF_HEADS), :].astype(BF16)
        acc = (jnp.dot(p_p.astype(BF16), v_p, preferred_element_type=F32)
               + jnp.dot(p_n.astype(BF16), vn_ref[:, sl], preferred_element_type=F32))
        o_ref[:, sl] = _diff_finish(acc, l, n, lam, sub_ref[...], lambda_init).astype(BF16)


def _diff_attention_sample(lams, subln, dq, kt_past, v_past, dkb, dvb, batch, seq, lambda_init):
    past = kt_past.shape[2]
    vec = _const_spec((1, DIFF_DK))
    new = pl.BlockSpec((seq, SEG), lambda b: (b, 0))
    return pl.pallas_call(
        functools.partial(_attn_sample_kernel, lambda_init=lambda_init),
        grid=(batch,),
        in_specs=[vec] * 4 + [_const_spec((1, HEAD)), new,
                  pl.BlockSpec((1, SEG, past), lambda b: (b, 0, 0)),
                  pl.BlockSpec((1, DIFF_HEADS * past, HEAD), lambda b: (b, 0, 0)), new, new],
        out_specs=new,
        out_shape=jax.ShapeDtypeStruct((batch * seq, SEG), BF16),
        compiler_params=_params("parallel"),
        name="diff_attn_sample",
    )(*lams, subln, dq, kt_past, v_past, dkb, dvb)


def _rms(x, w):
    return x * lax.rsqrt(jnp.mean(x * x, axis=-1, keepdims=True) + NORM_EPS) * w


def _mlp_kernel(ret_ref, dif_ref, x_ref, wo_ref, n1_ref, n2_ref, n3_ref, wu_ref, wd_ref, y_ref, u_ref,
                *, ff_blk):
    tm = x_ref.shape[0]
    halves = (slice(0, tm // 2), slice(tm // 2, tm))

    def front(rows):
        mix = (jnp.dot(ret_ref[rows, :], wo_ref[:SEG, :], preferred_element_type=F32)
               + jnp.dot(dif_ref[rows, :], wo_ref[SEG:, :], preferred_element_type=F32))
        x1 = x_ref[rows, :] + _rms(mix, n1_ref[...])
        return x1, _rms(x1, n2_ref[...]).astype(BF16)

    def up(h, rows, c):
        sl = slice(c * ff_blk, (c + 1) * ff_blk)
        u = jnp.maximum(jnp.dot(h, wu_ref[:, sl], preferred_element_type=F32), 0.0)
        u_ref[rows, sl] = (u * u).astype(BF16)

    (x1_lo, h_lo), (x1_hi, h_hi) = front(halves[0]), front(halves[1])
    up(h_lo, halves[0], 0)
    up(h_hi, halves[1], 0)
    h = jnp.concatenate([h_lo, h_hi], axis=0)
    for c in range(1, D_FF // ff_blk):
        up(h, slice(None), c)
    for rows, x1 in zip(halves, (x1_lo, x1_hi)):
        f = jnp.dot(u_ref[rows, :], wd_ref[...], preferred_element_type=F32)
        y_ref[rows, :] = x1 + _rms(f, n3_ref[...])


def _out_mlp(ret_out, diff_out, x, w_out, n1, n2, n3, w_up, w_down, tm, ff_blk=1024):
    t = x.shape[0]
    half = pl.BlockSpec((tm, SEG), lambda i: (i, 0))
    row = pl.BlockSpec((tm, D_MODEL), lambda i: (i, 0))
    nrm = _const_spec((1, D_MODEL))
    return pl.pallas_call(
        functools.partial(_mlp_kernel, ff_blk=ff_blk),
        grid=(t // tm,),
        in_specs=[half, half, row, _const_spec((D_MODEL, D_MODEL)), nrm, nrm, nrm,
                  _const_spec((D_MODEL, D_FF)), _const_spec((D_FF, D_MODEL))],
        out_specs=row,
        out_shape=jax.ShapeDtypeStruct((t, D_MODEL), F32),
        scratch_shapes=[pltpu.VMEM((tm, D_FF), BF16)],
        compiler_params=_params("parallel"),
        name="out_mlp",
    )(ret_out, diff_out, x, w_out, n1, n2, n3, w_up, w_down)


def _layer(x, pos0, state0, k_past, v_past, w, lambda_init, tm_proj, tm, ret_blk):
    batch, seq, _ = x.shape
    prompt = k_past is None
    xf = x.reshape(batch * seq, D_MODEL)
    tables = _rope_tables(max(seq, tm_proj), seq, pos0)
    rq, rk, rv, rg, dq, dk, dv, dkb, dvb = _project(xf, w["norm_mix_pre"], w["w_in"], tables, tm_proj, seq,
                                                    transposed=prompt)
    ret_out, new_state = _retention(rq, rk, rv, rg, state0, batch, seq, ret_blk,
                                    blocks_per_iter=RET_BLOCKS_PER_ITER if prompt else 1)
    lams = (w["lambda_q1"], w["lambda_k1"], w["lambda_q2"], w["lambda_k2"])
    if prompt:
        diff_out = _diff_attention_prompt(lams, w["diff_subln"].reshape(HEAD, 1), dq, dkb, dvb,
                                          batch, seq, lambda_init)
        k_new = dk.reshape(batch, 2 * DIFF_HEADS, DIFF_DK, seq).transpose(0, 3, 1, 2)
    else:
        past = k_past.shape[1]
        kt_past = k_past.transpose(0, 2, 3, 1).reshape(batch, SEG, past)
        diff_out = _diff_attention_sample(
            lams, w["diff_subln"], dq, kt_past, v_past.reshape(batch, past * DIFF_HEADS, HEAD),
            dkb, dvb, batch, seq, lambda_init)
        k_new = dk.reshape(batch, seq, 2 * DIFF_HEADS, DIFF_DK)
    y = _out_mlp(ret_out, diff_out, xf, w["w_out"], w["norm_mix_post"], w["norm_ffn_pre"],
                 w["norm_ffn_post"], w["w_up"], w["w_down"], tm)
    return (y.reshape(batch, seq, D_MODEL), k_new, dv.reshape(batch, seq, DIFF_HEADS, HEAD), new_state)


def kernel(x_prompt, x_sample, cache_diff_k, cache_diff_v, state_ret, norm_mix_pre, norm_mix_post, w_in, diff_subln, lambda_q1, lambda_k1, lambda_q2, lambda_k2, w_out, norm_ffn_pre, norm_ffn_post, w_up, w_down):
    depth = w_in.shape[0]
    past = cache_diff_k.shape[2]
    yp, ys = x_prompt, x_sample
    outs = [[] for _ in range(6)]
    for l in range(depth):
        lambda_init = 0.8 - 0.6 * math.exp(-0.3 * l)
        w = dict(
            norm_mix_pre=norm_mix_pre[l][None], norm_mix_post=norm_mix_post[l][None],
            w_in=w_in[l].astype(BF16), diff_subln=diff_subln[l][None],
            lambda_q1=lambda_q1[l][None], lambda_k1=lambda_k1[l][None],
            lambda_q2=lambda_q2[l][None], lambda_k2=lambda_k2[l][None],
            w_out=w_out[l].astype(BF16), norm_ffn_pre=norm_ffn_pre[l][None],
            norm_ffn_post=norm_ffn_post[l][None],
            w_up=w_up[l].astype(BF16), w_down=w_down[l].astype(BF16))
        yp, kp, vp, sp = _layer(yp, 0, None, None, None, w, lambda_init,
                                tm_proj=TM_PROJ, tm=TM_MLP, ret_blk=RET_BLK)
        n_sample = x_sample.shape[0] * x_sample.shape[1]
        ys, ks, vs, ss = _layer(ys, past, state_ret[l], cache_diff_k[l], cache_diff_v[l], w, lambda_init,
                                tm_proj=n_sample, tm=n_sample, ret_blk=x_sample.shape[1])
        for lst, val in zip(outs, (kp, vp, sp, ks, vs, ss)):
            lst.append(val)
    return (yp, ys) + tuple(jnp.stack(o) for o in outs)
```

```python
import functools
import math

import jax
import jax.numpy as jnp
from jax import lax
from jax.experimental import pallas as pl
from jax.experimental.pallas import tpu as pltpu

D_MODEL = 1024
CHUNK = 64
HEAD = 128
RET_HEADS = 4
DIFF_HEADS = 4
DIFF_DK = 64
SEG = 512
N_SEG = 7
D_IN = N_SEG * SEG
D_FF = 4 * D_MODEL
ROPE_THETA = 10000.0
NORM_EPS = 1e-6
SUBLN_EPS = 1e-5
NEG_INF = -1e30
TM_PROJ = 1024
TM_MLP = 1024
MLP_ROW_PARTS = 4
ATTN_BLK = 256
RET_BLK = 256
RET_BLOCKS_PER_ITER = 2
Q_SCALE = DIFF_DK ** -0.5 * math.log2(math.e)
ONES_ROWS = 16
VAUG = HEAD + ONES_ROWS
REGION_BLOCKS = 8
SKEW = 2

VMEM_LIMIT_BYTES = 56 * 1024 * 1024

F32 = jnp.float32
BF16 = jnp.bfloat16

_NT = (((1,), (1,)), ((), ()))
_TN = (((0,), (0,)), ((), ()))


def _params(*sem):
    return pltpu.CompilerParams(dimension_semantics=sem, vmem_limit_bytes=VMEM_LIMIT_BYTES)


def _const_spec(shape):
    return pl.BlockSpec(shape, lambda *_: (0,) * len(shape), pipeline_mode=pl.Buffered(1))


def _rope_table_kernel(c128_ref, s128_ref, c64_ref, s64_ref, *, seq, pos0):
    n = c128_ref.shape[0]
    lane = lax.broadcasted_iota(jnp.int32, (n, HEAD), 1)
    pos = (lax.broadcasted_iota(jnp.int32, (n, HEAD), 0) % seq + pos0).astype(F32)
    log_theta = math.log(ROPE_THETA)

    def table(d):
        half = d // 2
        r = lane % d
        i = (r % half).astype(F32)
        inv = jnp.exp(i * (-2.0 * log_theta / d))
        ang = pos * inv
        sign = jnp.where(r < half, -1.0, 1.0).astype(F32)
        return jnp.cos(ang), jnp.sin(ang) * sign

    c128_ref[...], s128_ref[...] = table(HEAD)
    c64_ref[...], s64_ref[...] = table(DIFF_DK)


def _rope_tables(n_rows, seq, pos0):
    shp = jax.ShapeDtypeStruct((n_rows, HEAD), F32)
    return pl.pallas_call(
        functools.partial(_rope_table_kernel, seq=seq, pos0=pos0),
        out_shape=(shp,) * 4,
        name="rope_tables",
    )()


def _proj_kernel(x_ref, nw_ref, w_ref, c128_ref, s128_ref, c64_ref, s64_ref,
                 rq_ref, rk_ref, rv_ref, rg_ref, dq_ref, dk_ref, dv_ref, dkb_ref, dvb_ref, *, transposed):
    tm = x_ref.shape[0]
    half_rows = tm // 2

    def norm_rows(rows):
        x = x_ref[rows, :]
        ms = jnp.mean(x * x, axis=-1, keepdims=True)
        return (x * lax.rsqrt(ms + NORM_EPS) * nw_ref[...]).astype(BF16)

    h_lo, h_hi = norm_rows(slice(0, half_rows)), norm_rows(slice(half_rows, tm))
    h = jnp.concatenate([h_lo, h_hi], axis=0)

    def seg(s, split=False):
        w = w_ref[:, s * SEG:(s + 1) * SEG]
        if split:
            return jnp.concatenate([jnp.dot(h_lo, w, preferred_element_type=F32),
                                    jnp.dot(h_hi, w, preferred_element_type=F32)], axis=0)
        return jnp.dot(h, w, preferred_element_type=F32)

    c128, s128 = c128_ref[...], s128_ref[...]
    c64, s64 = c64_ref[...], s64_ref[...]
    lane = lax.broadcasted_iota(jnp.int32, (tm, HEAD), 1)
    first_half64 = (lane % DIFF_DK) < (DIFF_DK // 2)

    def rope128(z):
        return z * c128 + pltpu.roll(z, HEAD // 2, 1) * s128

    def rope64(z):
        rot = jnp.where(first_half64, pltpu.roll(z, HEAD - DIFF_DK // 2, 1),
                        pltpu.roll(z, DIFF_DK // 2, 1))
        return z * c64 + rot * s64

    def store_blocks_t(ref, sl, zt, width):
        for blk in range(tm // width):
            ref[blk, sl, :] = zt[:, blk * width:(blk + 1) * width].astype(BF16)

    z = seg(6, split=True)
    for hd in range(DIFF_HEADS):
        dv_ref[pl.ds(hd, tm, stride=DIFF_HEADS), :] = z[:, hd * HEAD:(hd + 1) * HEAD]
    if transposed:
        for hd in range(SEG // HEAD):
            store_blocks_t(dvb_ref, slice(hd * VAUG, hd * VAUG + HEAD), z[:, hd * HEAD:(hd + 1) * HEAD].T,
                           ATTN_BLK)
            dvb_ref[:, hd * VAUG + HEAD:(hd + 1) * VAUG, :] = jnp.ones(
                (tm // ATTN_BLK, ONES_ROWS, ATTN_BLK), BF16)
    else:
        dvb_ref[...] = z.astype(BF16)
    z = seg(5)
    for hd in range(SEG // HEAD):
        sl = slice(hd * HEAD, (hd + 1) * HEAD)
        kr = rope64(z[:, sl])
        dkb_ref[:, sl] = kr.astype(BF16)
        if transposed:
            dk_ref[0, sl, :] = kr.T
        else:
            dk_ref[:, sl] = kr
    z = seg(4)
    for hd in range(SEG // HEAD):
        sl = slice(hd * HEAD, (hd + 1) * HEAD)
        qr = rope64(z[:, sl]) * Q_SCALE
        if transposed:
            store_blocks_t(dq_ref, sl, qr.T, ATTN_BLK)
        else:
            dq_ref[:, sl] = qr.astype(BF16)
    z = seg(1)
    for hd in range(SEG // HEAD):
        sl = slice(hd * HEAD, (hd + 1) * HEAD)
        kr = rope128(z[:, sl]) * (HEAD ** -0.5)
        if transposed:
            store_blocks_t(rk_ref, sl, kr.T, RET_BLK)
        else:
            rk_ref[:, sl] = kr.astype(BF16)
    z = seg(0)
    for hd in range(SEG // HEAD):
        sl = slice(hd * HEAD, (hd + 1) * HEAD)
        rq_ref[:, sl] = rope128(z[:, sl]).astype(BF16)
    rv_ref[...] = seg(2).astype(BF16)
    rg_ref[...] = seg(3).astype(BF16)


def _project(x, norm_w, w_in, tables, tm, seq, transposed):
    t = x.shape[0]
    nt = tables[0].shape[0] // tm
    row = pl.BlockSpec((tm, D_MODEL), lambda i: (i, 0))
    out = pl.BlockSpec((tm, SEG), lambda i: (i, 0))
    tab = pl.BlockSpec((tm, HEAD), lambda i: (i % nt, 0))
    b16 = jax.ShapeDtypeStruct((t, SEG), BF16)
    f32 = jax.ShapeDtypeStruct((t, SEG), F32)
    dv_spec = pl.BlockSpec((tm * DIFF_HEADS, HEAD), lambda i: (i, 0))
    dv_shape = jax.ShapeDtypeStruct((t * DIFF_HEADS, HEAD), F32)
    if transposed:
        per_seq = seq // tm
        def blk_t(rows, width):
            return (pl.BlockSpec((tm // width, rows, width), lambda i: (i, 0, 0)),
                    jax.ShapeDtypeStruct((t // width, rows, width), BF16))
        (rk_spec, rk_shape), (q_spec, q_shape) = blk_t(SEG, RET_BLK), blk_t(SEG, ATTN_BLK)
        v_spec, v_shape = blk_t(DIFF_HEADS * VAUG, ATTN_BLK)
        dk_t = pl.BlockSpec((1, SEG, tm), lambda i: (i // per_seq, 0, i % per_seq))
        out_specs = [out, rk_spec, out, out, q_spec, dk_t, dv_spec, out, v_spec]
        out_shape = [b16, rk_shape, b16, b16, q_shape, jax.ShapeDtypeStruct((t // seq, SEG, seq), F32),
                     dv_shape, b16, v_shape]
    else:
        out_specs = [out] * 6 + [dv_spec, out, out]
        out_shape = [b16, b16, b16, b16, b16, f32, dv_shape, b16, b16]
    return pl.pallas_call(
        functools.partial(_proj_kernel, transposed=transposed),
        grid=(t // tm,),
        in_specs=[row, _const_spec((1, D_MODEL)), _const_spec((D_MODEL, D_IN)), tab, tab, tab, tab],
        out_specs=out_specs,
        out_shape=out_shape,
        compiler_params=_params("parallel"),
        name="in_proj",
    )(x, norm_w, w_in, *tables)


def _log_gamma(h):
    return math.log1p(-(2.0 ** (-5.0 - h)))


def _ret_tables(dmat, qdec, kdec, blk, k_feature_major):
    rel = (lax.broadcasted_iota(jnp.int32, (blk, blk), 0)
           - lax.broadcasted_iota(jnp.int32, (blk, blk), 1))
    relf = jnp.maximum(rel, 0).astype(F32)
    row = lax.broadcasted_iota(jnp.int32, qdec.shape[1:], 0).astype(F32)
    along = lax.broadcasted_iota(jnp.int32, kdec.shape[1:], 1 if k_feature_major else 0).astype(F32)
    for h in range(RET_HEADS):
        lg = _log_gamma(h)
        dmat[h] = jnp.where(rel >= 0, jnp.exp(lg * relf), 0.0)
        qdec[h] = jnp.exp(lg * (row + 1.0))
        kdec[h] = jnp.exp(lg * (blk - 1.0 - along))


def _ret_head(idx, h, q_ref, k_ref, v_ref, g_ref, o_ref, state, dmat, qdec, kdec, blk, k_feature_major):
    rows = pl.ds(pl.multiple_of(idx * blk, blk), blk)
    sl = slice(h * HEAD, (h + 1) * HEAD)
    q, v = q_ref[rows, sl], v_ref[rows, sl]
    st = state[h]
    if k_feature_major:
        kt = k_ref[idx, sl, :]
        s = jnp.dot(q, kt, preferred_element_type=F32)
        kd = (kt.astype(F32) * kdec[h]).astype(BF16)
        upd = jnp.dot(kd, v, preferred_element_type=F32)
    else:
        k = k_ref[rows, sl]
        s = lax.dot_general(q, k, _NT, preferred_element_type=F32)
        kd = (k.astype(F32) * kdec[h]).astype(BF16)
        upd = lax.dot_general(kd, v, _TN, preferred_element_type=F32)
    o = jnp.dot((s * dmat[h]).astype(BF16), v, preferred_element_type=F32)
    o = o + jnp.dot(q, st.astype(BF16), preferred_element_type=F32) * qdec[h]
    state[h] = math.exp(_log_gamma(h) * blk) * st + upd
    ro = o * lax.rsqrt(jnp.mean(o * o, axis=-1, keepdims=True) + NORM_EPS)
    half_g = 0.5 * g_ref[rows, sl].astype(F32)
    o_ref[rows, sl] = ((half_g + half_g * jnp.tanh(half_g)) * ro).astype(BF16)


def _ret_kernel(*refs, blk, has_init, k_feature_major, blocks_per_iter):
    if has_init:
        q_ref, k_ref, v_ref, g_ref, s0_ref, o_ref, sout_ref, state, dmat, qdec, kdec = refs
    else:
        q_ref, k_ref, v_ref, g_ref, o_ref, sout_ref, state, dmat, qdec, kdec = refs

    @pl.when(pl.program_id(0) == 0)
    def _():
        _ret_tables(dmat, qdec, kdec, blk, k_feature_major)

    if has_init:
        state[...] = s0_ref[0]
    else:
        state[...] = jnp.zeros_like(state)

    def group(g, carry):
        for sub in range(blocks_per_iter):
            for h in range(RET_HEADS):
                _ret_head(g * blocks_per_iter + sub, h, q_ref, k_ref, v_ref, g_ref, o_ref,
                          state, dmat, qdec, kdec, blk, k_feature_major)
        return carry

    lax.fori_loop(0, q_ref.shape[0] // (blk * blocks_per_iter), group, 0)
    sout_ref[0] = state[...]


def _retention(rq, rk, rv, rg, state0, batch, seq, blk, blocks_per_iter):
    tok = pl.BlockSpec((seq, SEG), lambda b: (b, 0))
    st_spec = pl.BlockSpec((1, RET_HEADS, HEAD, HEAD), lambda b: (b, 0, 0, 0))
    has_init = state0 is not None
    k_feature_major = rk.ndim == 3
    k_spec = pl.BlockSpec((seq // blk, SEG, blk), lambda b: (b, 0, 0)) if k_feature_major else tok
    args = (rq, rk, rv, rg) + ((state0,) if has_init else ())
    return pl.pallas_call(
        functools.partial(_ret_kernel, blk=blk, has_init=has_init, k_feature_major=k_feature_major,
                          blocks_per_iter=blocks_per_iter),
        grid=(batch,),
        in_specs=[tok, k_spec, tok, tok] + ([st_spec] if has_init else []),
        out_specs=[tok, st_spec],
        out_shape=[jax.ShapeDtypeStruct((batch * seq, SEG), BF16),
                   jax.ShapeDtypeStruct((batch, RET_HEADS, HEAD, HEAD), F32)],
        scratch_shapes=[pltpu.VMEM((RET_HEADS, HEAD, HEAD), F32),
                        pltpu.VMEM((RET_HEADS, blk, blk), F32),
                        pltpu.VMEM((RET_HEADS, blk, HEAD), F32),
                        pltpu.VMEM((RET_HEADS, 1, blk) if k_feature_major else (RET_HEADS, blk, HEAD), F32)],
        compiler_params=_params("arbitrary"),
        name="retention",
    )(*args)


def _lambda(lq1_ref, lk1_ref, lq2_ref, lk2_ref, lambda_init):
    a = jnp.sum(lq1_ref[...] * lk1_ref[...], axis=-1, keepdims=True)
    b = jnp.sum(lq2_ref[...] * lk2_ref[...], axis=-1, keepdims=True)
    return jnp.exp(a) - jnp.exp(b) + lambda_init


def _stack_heads(q):
    lane = lax.broadcasted_iota(jnp.int32, q.shape, 1)
    zero = jnp.zeros_like(q)
    return jnp.concatenate([jnp.where(lane < DIFF_DK, q, zero),
                            jnp.where(lane >= DIFF_DK, q, zero)], axis=0)


def _diff_finish(acc, l, n, lam, sub, lambda_init):
    o = acc[:n] * (1.0 / l[:n]) - lam * (acc[n:] * (1.0 / l[n:]))
    o = o * lax.rsqrt(jnp.mean(o * o, axis=-1, keepdims=True) + SUBLN_EPS)
    return o * sub * (1.0 - lambda_init)


def _attn_kernel(lq1_ref, lk1_ref, lq2_ref, lk2_ref, sub_ref, qt_ref, k_ref, vt_ref, o_ref,
                 qs_ref, m_ref, acc_ref, *, lambda_init):
    lam = _lambda(lq1_ref, lk1_ref, lq2_ref, lk2_ref, lambda_init)
    out_gain = sub_ref[...] * (1.0 - lambda_init)

    def query_block(qi, carry):
        _attn_query_block(qi, lam, out_gain, qt_ref, k_ref, vt_ref, o_ref, qs_ref, m_ref, acc_ref)
        return carry

    lax.fori_loop(0, qt_ref.shape[0], query_block, 0)


def _attn_query_block(qi, lam, out_gain, qt_ref, k_ref, vt_ref, o_ref, qs_ref, m_ref, acc_ref):
    blk = ATTN_BLK
    row = lax.broadcasted_iota(jnp.int32, (HEAD, blk), 0)
    for h in range(DIFF_HEADS):
        qt = qt_ref[qi, h * HEAD:(h + 1) * HEAD, :]
        zero = jnp.zeros_like(qt)
        qs_ref[h, :, :blk] = jnp.where(row < DIFF_DK, qt, zero)
        qs_ref[h, :, blk:] = jnp.where(row >= DIFF_DK, qt, zero)
    m_ref[...] = jnp.full_like(m_ref, NEG_INF)
    acc_ref[...] = jnp.zeros_like(acc_ref)
    out_rows = pl.ds(pl.multiple_of(qi * blk, blk), blk)

    def finish(h):
        o = acc_ref[h, :HEAD, :] * (1.0 / acc_ref[h, HEAD:HEAD + 1, :])
        o = o[:, :blk] - lam * o[:, blk:]
        o = o * lax.rsqrt(jnp.mean(o * o, axis=0, keepdims=True) + SUBLN_EPS) * out_gain
        o_ref[out_rows, h * HEAD:(h + 1) * HEAD] = o.T.astype(BF16)

    def run(blocks, last=False):
        items = [(j, masked, h) for j, masked in blocks for h in range(DIFF_HEADS)]
        m_cur = [m_ref[h] for h in range(DIFF_HEADS)]
        if any(masked for _, masked in blocks):
            key = lax.broadcasted_iota(jnp.int32, (blk, 2 * blk), 0)
            qry = lax.broadcasted_iota(jnp.int32, (blk, 2 * blk), 1) % blk
            visible = key // CHUNK <= qry // CHUNK

        def scores(j, masked, h):
            start = pl.multiple_of(j * blk, blk)
            s = jnp.dot(k_ref[pl.ds(start, blk), h * HEAD:(h + 1) * HEAD], qs_ref[h],
                        preferred_element_type=F32)
            if masked:
                s = jnp.where(visible, s, NEG_INF)
            m_prev = m_cur[h]
            m_cur[h] = jnp.maximum(m_prev, jnp.max(s, axis=0, keepdims=True))
            return j, h, s, m_prev, m_cur[h]

        def accumulate(j, h, s, m_prev, m_new):
            alpha = jnp.exp2(m_prev - m_new)
            p = jnp.exp2(s - m_new).astype(BF16)
            pv = jnp.dot(vt_ref[j, h * VAUG:(h + 1) * VAUG, :], p, preferred_element_type=F32)
            acc_ref[h] = alpha * acc_ref[h] + pv

        pending = [scores(*it) for it in items[:SKEW]]
        for i in range(len(items)):
            current = pending.pop(0)
            if i + SKEW < len(items):
                pending.append(scores(*items[i + SKEW]))
            accumulate(*current)
            if last and i >= len(items) - DIFF_HEADS:
                finish(current[1])
        if not last:
            for h in range(DIFF_HEADS):
                m_ref[h] = m_cur[h]

    def body(t, carry):
        run([(REGION_BLOCKS * t + i, False) for i in range(REGION_BLOCKS)])
        return carry

    if qt_ref.shape[0] > REGION_BLOCKS:
        lax.fori_loop(0, qi // REGION_BLOCKS, body, 0)

    for rest in range(min(REGION_BLOCKS, qt_ref.shape[0])):
        @pl.when(qi % REGION_BLOCKS == rest)
        def _(rest=rest):
            run([(qi - rest + i, False) for i in range(rest)] + [(qi, True)], last=True)


def _diff_attention_prompt(lams, subln_col, dqt, dkb, dvt, batch, seq, lambda_init):
    blk = ATTN_BLK
    nq = seq // blk
    vec = _const_spec((1, DIFF_DK))
    return pl.pallas_call(
        functools.partial(_attn_kernel, lambda_init=lambda_init),
        grid=(batch,),
        in_specs=[vec] * 4 + [_const_spec((HEAD, 1)),
                  pl.BlockSpec((nq, SEG, blk), lambda b: (b, 0, 0)),
                  pl.BlockSpec((seq, SEG), lambda b: (b, 0)),
                  pl.BlockSpec((nq, DIFF_HEADS * VAUG, blk), lambda b: (b, 0, 0))],
        out_specs=pl.BlockSpec((seq, SEG), lambda b: (b, 0)),
        out_shape=jax.ShapeDtypeStruct((batch * seq, SEG), BF16),
        scratch_shapes=[pltpu.VMEM((DIFF_HEADS, HEAD, 2 * blk), BF16),
                        pltpu.VMEM((DIFF_HEADS, 1, 2 * blk), F32),
                        pltpu.VMEM((DIFF_HEADS, VAUG, 2 * blk), F32)],
        compiler_params=_params("parallel"),
        name="diff_attn_prompt",
    )(*lams, subln_col, dqt, dkb, dvt)


def _attn_sample_kernel(lq1_ref, lk1_ref, lq2_ref, lk2_ref, sub_ref, q_ref, kpt_ref, vp_ref,
                        kn_ref, vn_ref, o_ref, *, lambda_init):
    n = q_ref.shape[0]
    past = kpt_ref.shape[2]
    lam = _lambda(lq1_ref, lk1_ref, lq2_ref, lk2_ref, lambda_init)
    for h in range(DIFF_HEADS):
        sl = slice(h * HEAD, (h + 1) * HEAD)
        qs = _stack_heads(q_ref[:, sl])
        s_p = jnp.dot(qs, kpt_ref[0, sl, :].astype(BF16), preferred_element_type=F32)
        s_n = lax.dot_general(qs, kn_ref[:, sl], _NT, preferred_element_type=F32)
        m = jnp.maximum(jnp.max(s_p, axis=-1, keepdims=True), jnp.max(s_n, axis=-1, keepdims=True))
        p_p = jnp.exp2(s_p - m)
        p_n = jnp.exp2(s_n - m)
        l = jnp.sum(p_p, axis=-1, keepdims=True) + jnp.sum(p_n, axis=-1, keepdims=True)
        v_p = vp_ref[0, pl.ds(h, past, stride=DIFF_HEADS), :].astype(BF16)
        acc = (jnp.dot(p_p.astype(BF16), v_p, preferred_element_type=F32)
               + jnp.dot(p_n.astype(BF16), vn_ref[:, sl], preferred_element_type=F32))
        o_ref[:, sl] = _diff_finish(acc, l, n, lam, sub_ref[...], lambda_init).astype(BF16)


def _diff_attention_sample(lams, subln, dq, kt_past, v_past, dkb, dvb, batch, seq, lambda_init):
    past = kt_past.shape[2]
    vec = _const_spec((1, DIFF_DK))
    new = pl.BlockSpec((seq, SEG), lambda b: (b, 0))
    return pl.pallas_call(
        functools.partial(_attn_sample_kernel, lambda_init=lambda_init),
        grid=(batch,),
        in_specs=[vec] * 4 + [_const_spec((1, HEAD)), new,
                  pl.BlockSpec((1, SEG, past), lambda b: (b, 0, 0)),
                  pl.BlockSpec((1, DIFF_HEADS * past, HEAD), lambda b: (b, 0, 0)), new, new],
        out_specs=new,
        out_shape=jax.ShapeDtypeStruct((batch * seq, SEG), BF16),
        compiler_params=_params("parallel"),
        name="diff_attn_sample",
    )(*lams, subln, dq, kt_past, v_past, dkb, dvb)


def _rms(x, w):
    return x * lax.rsqrt(jnp.mean(x * x, axis=-1, keepdims=True) + NORM_EPS) * w


def _mlp_kernel(ret_ref, dif_ref, x_ref, wo_ref, n1_ref, n2_ref, n3_ref, wu_ref, wd_ref, y_ref, u_ref,
                *, ff_blk):
    tm = x_ref.shape[0]
    part = tm // MLP_ROW_PARTS
    parts = [slice(i * part, (i + 1) * part) for i in range(MLP_ROW_PARTS)]

    def front(rows):
        mix = (jnp.dot(ret_ref[rows, :], wo_ref[:SEG, :], preferred_element_type=F32)
               + jnp.dot(dif_ref[rows, :], wo_ref[SEG:, :], preferred_element_type=F32))
        x1 = x_ref[rows, :] + _rms(mix, n1_ref[...])
        return x1, _rms(x1, n2_ref[...]).astype(BF16)

    def up(h, rows, c):
        sl = slice(c * ff_blk, (c + 1) * ff_blk)
        u = jnp.maximum(jnp.dot(h, wu_ref[:, sl], preferred_element_type=F32), 0.0)
        u_ref[rows, sl] = (u * u).astype(BF16)

    fronts = [front(rows) for rows in parts]
    for rows, (_, h_part) in zip(parts, fronts):
        up(h_part, rows, 0)
    h = jnp.concatenate([h_part for _, h_part in fronts], axis=0)
    for c in range(1, D_FF // ff_blk):
        up(h, slice(None), c)
    for rows, (x1, _) in zip(parts, fronts):
        f = jnp.dot(u_ref[rows, :], wd_ref[...], preferred_element_type=F32)
        y_ref[rows, :] = x1 + _rms(f, n3_ref[...])


def _out_mlp(ret_out, diff_out, x, w_out, n1, n2, n3, w_up, w_down, tm, ff_blk=1024):
    t = x.shape[0]
    half = pl.BlockSpec((tm, SEG), lambda i: (i, 0))
    row = pl.BlockSpec((tm, D_MODEL), lambda i: (i, 0))
    nrm = _const_spec((1, D_MODEL))
    return pl.pallas_call(
        functools.partial(_mlp_kernel, ff_blk=ff_blk),
        grid=(t // tm,),
        in_specs=[half, half, row, _const_spec((D_MODEL, D_MODEL)), nrm, nrm, nrm,
                  _const_spec((D_MODEL, D_FF)), _const_spec((D_FF, D_MODEL))],
        out_specs=row,
        out_shape=jax.ShapeDtypeStruct((t, D_MODEL), F32),
        scratch_shapes=[pltpu.VMEM((tm, D_FF), BF16)],
        compiler_params=_params("parallel"),
        name="out_mlp",
    )(ret_out, diff_out, x, w_out, n1, n2, n3, w_up, w_down)


def _layer(x, pos0, state0, k_past, v_past, w, lambda_init, tm_proj, tm, ret_blk):
    batch, seq, _ = x.shape
    prompt = k_past is None
    xf = x.reshape(batch * seq, D_MODEL)
    tables = _rope_tables(max(seq, tm_proj), seq, pos0)
    rq, rk, rv, rg, dq, dk, dv, dkb, dvb = _project(xf, w["norm_mix_pre"], w["w_in"], tables, tm_proj, seq,
                                                    transposed=prompt)
    ret_out, new_state = _retention(rq, rk, rv, rg, state0, batch, seq, ret_blk,
                                    blocks_per_iter=RET_BLOCKS_PER_ITER if prompt else 1)
    lams = (w["lambda_q1"], w["lambda_k1"], w["lambda_q2"], w["lambda_k2"])
    if prompt:
        diff_out = _diff_attention_prompt(lams, w["diff_subln"].reshape(HEAD, 1), dq, dkb, dvb,
                                          batch, seq, lambda_init)
        k_new = dk.reshape(batch, 2 * DIFF_HEADS, DIFF_DK, seq).transpose(0, 3, 1, 2)
    else:
        past = k_past.shape[1]
        kt_past = k_past.transpose(0, 2, 3, 1).reshape(batch, SEG, past)
        diff_out = _diff_attention_sample(
            lams, w["diff_subln"], dq, kt_past, v_past.reshape(batch, past * DIFF_HEADS, HEAD),
            dkb, dvb, batch, seq, lambda_init)
        k_new = dk.reshape(batch, seq, 2 * DIFF_HEADS, DIFF_DK)
    y = _out_mlp(ret_out, diff_out, xf, w["w_out"], w["norm_mix_post"], w["norm_ffn_pre"],
                 w["norm_ffn_post"], w["w_up"], w["w_down"], tm)
    return (y.reshape(batch, seq, D_MODEL), k_new, dv.reshape(batch, seq, DIFF_HEADS, HEAD), new_state)


def kernel(x_prompt, x_sample, cache_diff_k, cache_diff_v, state_ret, norm_mix_pre, norm_mix_post, w_in, diff_subln, lambda_q1, lambda_k1, lambda_q2, lambda_k2, w_out, norm_ffn_pre, norm_ffn_post, w_up, w_down):
    depth = w_in.shape[0]
    past = cache_diff_k.shape[2]
    yp, ys = x_prompt, x_sample
    outs = [[] for _ in range(6)]
    for l in range(depth):
        lambda_init = 0.8 - 0.6 * math.exp(-0.3 * l)
        w = dict(
            norm_mix_pre=norm_mix_pre[l][None], norm_mix_post=norm_mix_post[l][None],
            w_in=w_in[l].astype(BF16), diff_subln=diff_subln[l][None],
            lambda_q1=lambda_q1[l][None], lambda_k1=lambda_k1[l][None],
            lambda_q2=lambda_q2[l][None], lambda_k2=lambda_k2[l][None],
            w_out=w_out[l].astype(BF16), norm_ffn_pre=norm_ffn_pre[l][None],
            norm_ffn_post=norm_ffn_post[l][None],
            w_up=w_up[l].astype(BF16), w_down=w_down[l].astype(BF16))
        yp, kp, vp, sp = _layer(yp, 0, None, None, None, w, lambda_init,
                                tm_proj=TM_PROJ, tm=TM_MLP, ret_blk=RET_BLK)
        n_sample = x_sample.shape[0] * x_sample.shape[1]
        ys, ks, vs, ss = _layer(ys, past, state_ret[l], cache_diff_k[l], cache_diff_v[l], w, lambda_init,
                                tm_proj=n_sample, tm=n_sample, ret_blk=x_sample.shape[1])
        for lst, val in zip(outs, (kp, vp, sp, ks, vs, ss)):
            lst.append(val)
    return (yp, ys) + tuple(jnp.stack(o) for o in outs)
```

```python
import functools
import math

import jax
import jax.numpy as jnp
from jax import lax
from jax.experimental import pallas as pl
from jax.experimental.pallas import tpu as pltpu

D_MODEL = 1024
CHUNK = 64
HEAD = 128
RET_HEADS = 4
DIFF_HEADS = 4
DIFF_DK = 64
SEG = 512
N_SEG = 7
D_IN = N_SEG * SEG
D_FF = 4 * D_MODEL
ROPE_THETA = 10000.0
NORM_EPS = 1e-6
SUBLN_EPS = 1e-5
NEG_INF = -1e30
TM_PROJ = 1024
TM_MLP = 1024
MLP_ROW_PARTS = 4
MLP_MIN_PART_ROWS = 128
ATTN_BLK = 256
RET_BLK = 256
RET_BLOCKS_PER_ITER = 8
Q_SCALE = DIFF_DK ** -0.5 * math.log2(math.e)
ONES_ROWS = 16
VAUG = HEAD + ONES_ROWS
REGION_BLOCKS = 8
SKEW = 2

VMEM_LIMIT_BYTES = 56 * 1024 * 1024

F32 = jnp.float32
BF16 = jnp.bfloat16

_NT = (((1,), (1,)), ((), ()))
_TN = (((0,), (0,)), ((), ()))


def _params(*sem):
    return pltpu.CompilerParams(dimension_semantics=sem, vmem_limit_bytes=VMEM_LIMIT_BYTES)


def _const_spec(shape):
    return pl.BlockSpec(shape, lambda *_: (0,) * len(shape), pipeline_mode=pl.Buffered(1))


def _rope_table_kernel(c128_ref, s128_ref, c64_ref, s64_ref, *, seq, pos0):
    n = c128_ref.shape[0]
    lane = lax.broadcasted_iota(jnp.int32, (n, HEAD), 1)
    pos = (lax.broadcasted_iota(jnp.int32, (n, HEAD), 0) % seq + pos0).astype(F32)
    log_theta = math.log(ROPE_THETA)

    def table(d):
        half = d // 2
        r = lane % d
        i = (r % half).astype(F32)
        inv = jnp.exp(i * (-2.0 * log_theta / d))
        ang = pos * inv
        sign = jnp.where(r < half, -1.0, 1.0).astype(F32)
        return jnp.cos(ang), jnp.sin(ang) * sign

    c128_ref[...], s128_ref[...] = table(HEAD)
    c64_ref[...], s64_ref[...] = table(DIFF_DK)


def _rope_tables(n_rows, seq, pos0):
    shp = jax.ShapeDtypeStruct((n_rows, HEAD), F32)
    return pl.pallas_call(
        functools.partial(_rope_table_kernel, seq=seq, pos0=pos0),
        out_shape=(shp,) * 4,
        name="rope_tables",
    )()


def _proj_kernel(x_ref, nw_ref, w_ref, c128_ref, s128_ref, c64_ref, s64_ref,
                 rq_ref, rk_ref, rv_ref, rg_ref, dq_ref, dk_ref, dv_ref, dkb_ref, dvb_ref, *, transposed):
    tm = x_ref.shape[0]
    half_rows = tm // 2

    def norm_rows(rows):
        x = x_ref[rows, :]
        ms = jnp.mean(x * x, axis=-1, keepdims=True)
        return (x * lax.rsqrt(ms + NORM_EPS) * nw_ref[...]).astype(BF16)

    h_lo, h_hi = norm_rows(slice(0, half_rows)), norm_rows(slice(half_rows, tm))
    h = jnp.concatenate([h_lo, h_hi], axis=0)

    def seg(s, split=False):
        w = w_ref[:, s * SEG:(s + 1) * SEG]
        if split:
            return jnp.concatenate([jnp.dot(h_lo, w, preferred_element_type=F32),
                                    jnp.dot(h_hi, w, preferred_element_type=F32)], axis=0)
        return jnp.dot(h, w, preferred_element_type=F32)

    c128, s128 = c128_ref[...], s128_ref[...]
    c64, s64 = c64_ref[...], s64_ref[...]
    lane = lax.broadcasted_iota(jnp.int32, (tm, HEAD), 1)
    first_half64 = (lane % DIFF_DK) < (DIFF_DK // 2)

    def rope128(z):
        return z * c128 + pltpu.roll(z, HEAD // 2, 1) * s128

    def rope64(z):
        rot = jnp.where(first_half64, pltpu.roll(z, HEAD - DIFF_DK // 2, 1),
                        pltpu.roll(z, DIFF_DK // 2, 1))
        return z * c64 + rot * s64

    def store_blocks_t(ref, sl, zt, width):
        for blk in range(tm // width):
            ref[blk, sl, :] = zt[:, blk * width:(blk + 1) * width].astype(BF16)

    z = seg(6, split=True)
    for hd in range(DIFF_HEADS):
        dv_ref[pl.ds(hd, tm, stride=DIFF_HEADS), :] = z[:, hd * HEAD:(hd + 1) * HEAD]
    if transposed:
        for hd in range(SEG // HEAD):
            store_blocks_t(dvb_ref, slice(hd * VAUG, hd * VAUG + HEAD), z[:, hd * HEAD:(hd + 1) * HEAD].T,
                           ATTN_BLK)
            dvb_ref[:, hd * VAUG + HEAD:(hd + 1) * VAUG, :] = jnp.ones(
                (tm // ATTN_BLK, ONES_ROWS, ATTN_BLK), BF16)
    else:
        dvb_ref[...] = z.astype(BF16)
    z = seg(5)
    for hd in range(SEG // HEAD):
        sl = slice(hd * HEAD, (hd + 1) * HEAD)
        kr = rope64(z[:, sl])
        dkb_ref[:, sl] = kr.astype(BF16)
        if transposed:
            dk_ref[0, sl, :] = kr.T
        else:
            dk_ref[:, sl] = kr
    z = seg(4)
    for hd in range(SEG // HEAD):
        sl = slice(hd * HEAD, (hd + 1) * HEAD)
        qr = rope64(z[:, sl]) * Q_SCALE
        if transposed:
            store_blocks_t(dq_ref, sl, qr.T, ATTN_BLK)
        else:
            dq_ref[:, sl] = qr.astype(BF16)
    z = seg(1)
    for hd in range(SEG // HEAD):
        sl = slice(hd * HEAD, (hd + 1) * HEAD)
        kr = rope128(z[:, sl]) * (HEAD ** -0.5)
        if transposed:
            store_blocks_t(rk_ref, sl, kr.T, RET_BLK)
        else:
            rk_ref[:, sl] = kr.astype(BF16)
    z = seg(0)
    for hd in range(SEG // HEAD):
        sl = slice(hd * HEAD, (hd + 1) * HEAD)
        rq_ref[:, sl] = rope128(z[:, sl]).astype(BF16)
    rv_ref[...] = seg(2).astype(BF16)
    rg_ref[...] = seg(3).astype(BF16)


def _project(x, norm_w, w_in, tables, tm, seq, transposed):
    t = x.shape[0]
    nt = tables[0].shape[0] // tm
    row = pl.BlockSpec((tm, D_MODEL), lambda i: (i, 0))
    out = pl.BlockSpec((tm, SEG), lambda i: (i, 0))
    tab = pl.BlockSpec((tm, HEAD), lambda i: (i % nt, 0))
    b16 = jax.ShapeDtypeStruct((t, SEG), BF16)
    f32 = jax.ShapeDtypeStruct((t, SEG), F32)
    dv_spec = pl.BlockSpec((tm * DIFF_HEADS, HEAD), lambda i: (i, 0))
    dv_shape = jax.ShapeDtypeStruct((t * DIFF_HEADS, HEAD), F32)
    if transposed:
        per_seq = seq // tm
        def blk_t(rows, width):
            return (pl.BlockSpec((tm // width, rows, width), lambda i: (i, 0, 0)),
                    jax.ShapeDtypeStruct((t // width, rows, width), BF16))
        (rk_spec, rk_shape), (q_spec, q_shape) = blk_t(SEG, RET_BLK), blk_t(SEG, ATTN_BLK)
        v_spec, v_shape = blk_t(DIFF_HEADS * VAUG, ATTN_BLK)
        dk_t = pl.BlockSpec((1, SEG, tm), lambda i: (i // per_seq, 0, i % per_seq))
        out_specs = [out, rk_spec, out, out, q_spec, dk_t, dv_spec, out, v_spec]
        out_shape = [b16, rk_shape, b16, b16, q_shape, jax.ShapeDtypeStruct((t // seq, SEG, seq), F32),
                     dv_shape, b16, v_shape]
    else:
        out_specs = [out] * 6 + [dv_spec, out, out]
        out_shape = [b16, b16, b16, b16, b16, f32, dv_shape, b16, b16]
    return pl.pallas_call(
        functools.partial(_proj_kernel, transposed=transposed),
        grid=(t // tm,),
        in_specs=[row, _const_spec((1, D_MODEL)), _const_spec((D_MODEL, D_IN)), tab, tab, tab, tab],
        out_specs=out_specs,
        out_shape=out_shape,
        compiler_params=_params("parallel"),
        name="in_proj",
    )(x, norm_w, w_in, *tables)


def _log_gamma(h):
    return math.log1p(-(2.0 ** (-5.0 - h)))


def _ret_tables(dmat, qdec, kdec, blk, k_feature_major):
    rel = (lax.broadcasted_iota(jnp.int32, (blk, blk), 0)
           - lax.broadcasted_iota(jnp.int32, (blk, blk), 1))
    relf = jnp.maximum(rel, 0).astype(F32)
    row = lax.broadcasted_iota(jnp.int32, qdec.shape[1:], 0).astype(F32)
    along = lax.broadcasted_iota(jnp.int32, kdec.shape[1:], 1 if k_feature_major else 0).astype(F32)
    for h in range(RET_HEADS):
        lg = _log_gamma(h)
        dmat[h] = jnp.where(rel >= 0, jnp.exp(lg * relf), 0.0)
        qdec[h] = jnp.exp(lg * (row + 1.0))
        kdec[h] = jnp.exp(lg * (blk - 1.0 - along))


def _ret_head(idx, h, q_ref, k_ref, v_ref, g_ref, o_ref, state, dmat, qdec, kdec, blk, k_feature_major):
    rows = pl.ds(pl.multiple_of(idx * blk, blk), blk)
    sl = slice(h * HEAD, (h + 1) * HEAD)
    q, v = q_ref[rows, sl], v_ref[rows, sl]
    st = state[h]
    if k_feature_major:
        kt = k_ref[idx, sl, :]
        s = jnp.dot(q, kt, preferred_element_type=F32)
        kd = (kt.astype(F32) * kdec[h]).astype(BF16)
        upd = jnp.dot(kd, v, preferred_element_type=F32)
    else:
        k = k_ref[rows, sl]
        s = lax.dot_general(q, k, _NT, preferred_element_type=F32)
        kd = (k.astype(F32) * kdec[h]).astype(BF16)
        upd = lax.dot_general(kd, v, _TN, preferred_element_type=F32)
    o = jnp.dot((s * dmat[h]).astype(BF16), v, preferred_element_type=F32)
    o = o + jnp.dot(q, st.astype(BF16), preferred_element_type=F32) * qdec[h]
    state[h] = math.exp(_log_gamma(h) * blk) * st + upd
    ro = o * lax.rsqrt(jnp.mean(o * o, axis=-1, keepdims=True) + NORM_EPS)
    half_g = 0.5 * g_ref[rows, sl].astype(F32)
    o_ref[rows, sl] = ((half_g + half_g * jnp.tanh(half_g)) * ro).astype(BF16)


def _ret_kernel(*refs, blk, has_init, k_feature_major, blocks_per_iter):
    if has_init:
        q_ref, k_ref, v_ref, g_ref, s0_ref, o_ref, sout_ref, state, dmat, qdec, kdec = refs
    else:
        q_ref, k_ref, v_ref, g_ref, o_ref, sout_ref, state, dmat, qdec, kdec = refs

    @pl.when(pl.program_id(0) == 0)
    def _():
        _ret_tables(dmat, qdec, kdec, blk, k_feature_major)

    if has_init:
        state[...] = s0_ref[0]
    else:
        state[...] = jnp.zeros_like(state)

    def group(g, carry):
        for sub in range(blocks_per_iter):
            for h in range(RET_HEADS):
                _ret_head(g * blocks_per_iter + sub, h, q_ref, k_ref, v_ref, g_ref, o_ref,
                          state, dmat, qdec, kdec, blk, k_feature_major)
        return carry

    lax.fori_loop(0, q_ref.shape[0] // (blk * blocks_per_iter), group, 0)
    sout_ref[0] = state[...]


def _retention(rq, rk, rv, rg, state0, batch, seq, blk, blocks_per_iter):
    tok = pl.BlockSpec((seq, SEG), lambda b: (b, 0))
    st_spec = pl.BlockSpec((1, RET_HEADS, HEAD, HEAD), lambda b: (b, 0, 0, 0))
    has_init = state0 is not None
    k_feature_major = rk.ndim == 3
    k_spec = pl.BlockSpec((seq // blk, SEG, blk), lambda b: (b, 0, 0)) if k_feature_major else tok
    args = (rq, rk, rv, rg) + ((state0,) if has_init else ())
    return pl.pallas_call(
        functools.partial(_ret_kernel, blk=blk, has_init=has_init, k_feature_major=k_feature_major,
                          blocks_per_iter=blocks_per_iter),
        grid=(batch,),
        in_specs=[tok, k_spec, tok, tok] + ([st_spec] if has_init else []),
        out_specs=[tok, st_spec],
        out_shape=[jax.ShapeDtypeStruct((batch * seq, SEG), BF16),
                   jax.ShapeDtypeStruct((batch, RET_HEADS, HEAD, HEAD), F32)],
        scratch_shapes=[pltpu.VMEM((RET_HEADS, HEAD, HEAD), F32),
                        pltpu.VMEM((RET_HEADS, blk, blk), F32),
                        pltpu.VMEM((RET_HEADS, blk, HEAD), F32),
                        pltpu.VMEM((RET_HEADS, 1, blk) if k_feature_major else (RET_HEADS, blk, HEAD), F32)],
        compiler_params=_params("arbitrary"),
        name="retention",
    )(*args)


def _lambda(lq1_ref, lk1_ref, lq2_ref, lk2_ref, lambda_init):
    a = jnp.sum(lq1_ref[...] * lk1_ref[...], axis=-1, keepdims=True)
    b = jnp.sum(lq2_ref[...] * lk2_ref[...], axis=-1, keepdims=True)
    return jnp.exp(a) - jnp.exp(b) + lambda_init


def _stack_heads(q):
    lane = lax.broadcasted_iota(jnp.int32, q.shape, 1)
    zero = jnp.zeros_like(q)
    return jnp.concatenate([jnp.where(lane < DIFF_DK, q, zero),
                            jnp.where(lane >= DIFF_DK, q, zero)], axis=0)


def _diff_finish(acc, l, n, lam, sub, lambda_init):
    o = acc[:n] * (1.0 / l[:n]) - lam * (acc[n:] * (1.0 / l[n:]))
    o = o * lax.rsqrt(jnp.mean(o * o, axis=-1, keepdims=True) + SUBLN_EPS)
    return o * sub * (1.0 - lambda_init)


def _attn_kernel(lq1_ref, lk1_ref, lq2_ref, lk2_ref, sub_ref, qt_ref, k_ref, vt_ref, o_ref,
                 qs_ref, m_ref, acc_ref, *, lambda_init):
    lam = _lambda(lq1_ref, lk1_ref, lq2_ref, lk2_ref, lambda_init)
    out_gain = sub_ref[...] * (1.0 - lambda_init)

    def query_block(qi, carry):
        _attn_query_block(qi, lam, out_gain, qt_ref, k_ref, vt_ref, o_ref, qs_ref, m_ref, acc_ref)
        return carry

    lax.fori_loop(0, qt_ref.shape[0], query_block, 0)


def _attn_query_block(qi, lam, out_gain, qt_ref, k_ref, vt_ref, o_ref, qs_ref, m_ref, acc_ref):
    blk = ATTN_BLK
    row = lax.broadcasted_iota(jnp.int32, (HEAD, blk), 0)
    for h in range(DIFF_HEADS):
        qt = qt_ref[qi, h * HEAD:(h + 1) * HEAD, :]
        zero = jnp.zeros_like(qt)
        qs_ref[h, :, :blk] = jnp.where(row < DIFF_DK, qt, zero)
        qs_ref[h, :, blk:] = jnp.where(row >= DIFF_DK, qt, zero)
    m_ref[...] = jnp.full_like(m_ref, NEG_INF)
    acc_ref[...] = jnp.zeros_like(acc_ref)
    out_rows = pl.ds(pl.multiple_of(qi * blk, blk), blk)

    def finish(h):
        o = acc_ref[h, :HEAD, :] * (1.0 / acc_ref[h, HEAD:HEAD + 1, :])
        o = o[:, :blk] - lam * o[:, blk:]
        o = o * lax.rsqrt(jnp.mean(o * o, axis=0, keepdims=True) + SUBLN_EPS) * out_gain
        o_ref[out_rows, h * HEAD:(h + 1) * HEAD] = o.T.astype(BF16)

    def run(blocks, last=False):
        items = [(j, masked, h) for j, masked in blocks for h in range(DIFF_HEADS)]
        m_cur = [m_ref[h] for h in range(DIFF_HEADS)]
        if any(masked for _, masked in blocks):
            key = lax.broadcasted_iota(jnp.int32, (blk, 2 * blk), 0)
            qry = lax.broadcasted_iota(jnp.int32, (blk, 2 * blk), 1) % blk
            visible = key // CHUNK <= qry // CHUNK

        def scores(j, masked, h):
            start = pl.multiple_of(j * blk, blk)
            s = jnp.dot(k_ref[pl.ds(start, blk), h * HEAD:(h + 1) * HEAD], qs_ref[h],
                        preferred_element_type=F32)
            if masked:
                s = jnp.where(visible, s, NEG_INF)
            m_prev = m_cur[h]
            m_cur[h] = jnp.maximum(m_prev, jnp.max(s, axis=0, keepdims=True))
            return j, h, s, m_prev, m_cur[h]

        def accumulate(j, h, s, m_prev, m_new):
            alpha = jnp.exp2(m_prev - m_new)
            p = jnp.exp2(s - m_new).astype(BF16)
            pv = jnp.dot(vt_ref[j, h * VAUG:(h + 1) * VAUG, :], p, preferred_element_type=F32)
            acc_ref[h] = alpha * acc_ref[h] + pv

        pending = [scores(*it) for it in items[:SKEW]]
        for i in range(len(items)):
            current = pending.pop(0)
            if i + SKEW < len(items):
                pending.append(scores(*items[i + SKEW]))
            accumulate(*current)
            if last and i >= len(items) - DIFF_HEADS:
                finish(current[1])
        if not last:
            for h in range(DIFF_HEADS):
                m_ref[h] = m_cur[h]

    def body(t, carry):
        run([(REGION_BLOCKS * t + i, False) for i in range(REGION_BLOCKS)])
        return carry

    if qt_ref.shape[0] > REGION_BLOCKS:
        lax.fori_loop(0, qi // REGION_BLOCKS, body, 0)

    for rest in range(min(REGION_BLOCKS, qt_ref.shape[0])):
        @pl.when(qi % REGION_BLOCKS == rest)
        def _(rest=rest):
            run([(qi - rest + i, False) for i in range(rest)] + [(qi, True)], last=True)


def _diff_attention_prompt(lams, subln_col, dqt, dkb, dvt, batch, seq, lambda_init):
    blk = ATTN_BLK
    nq = seq // blk
    vec = _const_spec((1, DIFF_DK))
    return pl.pallas_call(
        functools.partial(_attn_kernel, lambda_init=lambda_init),
        grid=(batch,),
        in_specs=[vec] * 4 + [_const_spec((HEAD, 1)),
                  pl.BlockSpec((nq, SEG, blk), lambda b: (b, 0, 0)),
                  pl.BlockSpec((seq, SEG), lambda b: (b, 0)),
                  pl.BlockSpec((nq, DIFF_HEADS * VAUG, blk), lambda b: (b, 0, 0))],
        out_specs=pl.BlockSpec((seq, SEG), lambda b: (b, 0)),
        out_shape=jax.ShapeDtypeStruct((batch * seq, SEG), BF16),
        scratch_shapes=[pltpu.VMEM((DIFF_HEADS, HEAD, 2 * blk), BF16),
                        pltpu.VMEM((DIFF_HEADS, 1, 2 * blk), F32),
                        pltpu.VMEM((DIFF_HEADS, VAUG, 2 * blk), F32)],
        compiler_params=_params("parallel"),
        name="diff_attn_prompt",
    )(*lams, subln_col, dqt, dkb, dvt)


def _attn_sample_kernel(lq1_ref, lk1_ref, lq2_ref, lk2_ref, sub_ref, q_ref, kpt_ref, vp_ref,
                        kn_ref, vn_ref, o_ref, *, lambda_init):
    n = q_ref.shape[0]
    past = kpt_ref.shape[2]
    lam = _lambda(lq1_ref, lk1_ref, lq2_ref, lk2_ref, lambda_init)
    for h in range(DIFF_HEADS):
        sl = slice(h * HEAD, (h + 1) * HEAD)
        qs = _stack_heads(q_ref[:, sl])
        s_p = jnp.dot(qs, kpt_ref[0, sl, :].astype(BF16), preferred_element_type=F32)
        s_n = lax.dot_general(qs, kn_ref[:, sl], _NT, preferred_element_type=F32)
        m = jnp.maximum(jnp.max(s_p, axis=-1, keepdims=True), jnp.max(s_n, axis=-1, keepdims=True))
        p_p = jnp.exp2(s_p - m)
        p_n = jnp.exp2(s_n - m)
        l = jnp.sum(p_p, axis=-1, keepdims=True) + jnp.sum(p_n, axis=-1, keepdims=True)
        v_p = vp_ref[0, pl.ds(h, past, stride=DIFF_HEADS), :].astype(BF16)
        acc = (jnp.dot(p_p.astype(BF16), v_p, preferred_element_type=F32)
               + jnp.dot(p_n.astype(BF16), vn_ref[:, sl], preferred_element_type=F32))
        o_ref[:, sl] = _diff_finish(acc, l, n, lam, sub_ref[...], lambda_init).astype(BF16)


def _diff_attention_sample(lams, subln, dq, kt_past, v_past, dkb, dvb, batch, seq, lambda_init):
    past = kt_past.shape[2]
    vec = _const_spec((1, DIFF_DK))
    new = pl.BlockSpec((seq, SEG), lambda b: (b, 0))
    return pl.pallas_call(
        functools.partial(_attn_sample_kernel, lambda_init=lambda_init),
        grid=(batch,),
        in_specs=[vec] * 4 + [_const_spec((1, HEAD)), new,
                  pl.BlockSpec((1, SEG, past), lambda b: (b, 0, 0)),
                  pl.BlockSpec((1, DIFF_HEADS * past, HEAD), lambda b: (b, 0, 0)), new, new],
        out_specs=new,
        out_shape=jax.ShapeDtypeStruct((batch * seq, SEG), BF16),
        compiler_params=_params("parallel"),
        name="diff_attn_sample",
    )(*lams, subln, dq, kt_past, v_past, dkb, dvb)


def _rms(x, w):
    return x * lax.rsqrt(jnp.mean(x * x, axis=-1, keepdims=True) + NORM_EPS) * w


def _mlp_kernel(ret_ref, dif_ref, x_ref, wo_ref, n1_ref, n2_ref, n3_ref, wu_ref, wd_ref, y_ref, u_ref,
                *, ff_blk):
    tm = x_ref.shape[0]
    n_parts = max(1, min(MLP_ROW_PARTS, tm // MLP_MIN_PART_ROWS))
    part = tm // n_parts
    parts = [slice(i * part, (i + 1) * part) for i in range(n_parts)]

    def front(rows):
        mix = (jnp.dot(ret_ref[rows, :], wo_ref[:SEG, :], preferred_element_type=F32)
               + jnp.dot(dif_ref[rows, :], wo_ref[SEG:, :], preferred_element_type=F32))
        x1 = x_ref[rows, :] + _rms(mix, n1_ref[...])
        return x1, _rms(x1, n2_ref[...]).astype(BF16)

    def up(h, rows, c):
        sl = slice(c * ff_blk, (c + 1) * ff_blk)
        u = jnp.maximum(jnp.dot(h, wu_ref[:, sl], preferred_element_type=F32), 0.0)
        u_ref[rows, sl] = (u * u).astype(BF16)

    fronts = [front(rows) for rows in parts]
    for rows, (_, h_part) in zip(parts, fronts):
        up(h_part, rows, 0)
    h = jnp.concatenate([h_part for _, h_part in fronts], axis=0)
    for c in range(1, D_FF // ff_blk):
        up(h, slice(None), c)
    for rows, (x1, _) in zip(parts, fronts):
        f = jnp.dot(u_ref[rows, :], wd_ref[...], preferred_element_type=F32)
        y_ref[rows, :] = x1 + _rms(f, n3_ref[...])


def _out_mlp(ret_out, diff_out, x, w_out, n1, n2, n3, w_up, w_down, tm, ff_blk=1024):
    t = x.shape[0]
    half = pl.BlockSpec((tm, SEG), lambda i: (i, 0))
    row = pl.BlockSpec((tm, D_MODEL), lambda i: (i, 0))
    nrm = _const_spec((1, D_MODEL))
    return pl.pallas_call(
        functools.partial(_mlp_kernel, ff_blk=ff_blk),
        grid=(t // tm,),
        in_specs=[half, half, row, _const_spec((D_MODEL, D_MODEL)), nrm, nrm, nrm,
                  _const_spec((D_MODEL, D_FF)), _const_spec((D_FF, D_MODEL))],
        out_specs=row,
        out_shape=jax.ShapeDtypeStruct((t, D_MODEL), F32),
        scratch_shapes=[pltpu.VMEM((tm, D_FF), BF16)],
        compiler_params=_params("parallel"),
        name="out_mlp",
    )(ret_out, diff_out, x, w_out, n1, n2, n3, w_up, w_down)


def _layer(x, pos0, state0, k_past, v_past, w, lambda_init, tm_proj, tm, ret_blk):
    batch, seq, _ = x.shape
    prompt = k_past is None
    xf = x.reshape(batch * seq, D_MODEL)
    tables = _rope_tables(max(seq, tm_proj), seq, pos0)
    rq, rk, rv, rg, dq, dk, dv, dkb, dvb = _project(xf, w["norm_mix_pre"], w["w_in"], tables, tm_proj, seq,
                                                    transposed=prompt)
    ret_out, new_state = _retention(rq, rk, rv, rg, state0, batch, seq, ret_blk,
                                    blocks_per_iter=RET_BLOCKS_PER_ITER if prompt else 1)
    lams = (w["lambda_q1"], w["lambda_k1"], w["lambda_q2"], w["lambda_k2"])
    if prompt:
        diff_out = _diff_attention_prompt(lams, w["diff_subln"].reshape(HEAD, 1), dq, dkb, dvb,
                                          batch, seq, lambda_init)
        k_new = dk.reshape(batch, 2 * DIFF_HEADS, DIFF_DK, seq).transpose(0, 3, 1, 2)
    else:
        past = k_past.shape[1]
        kt_past = k_past.transpose(0, 2, 3, 1).reshape(batch, SEG, past)
        diff_out = _diff_attention_sample(
            lams, w["diff_subln"], dq, kt_past, v_past.reshape(batch, past * DIFF_HEADS, HEAD),
            dkb, dvb, batch, seq, lambda_init)
        k_new = dk.reshape(batch, seq, 2 * DIFF_HEADS, DIFF_DK)
    y = _out_mlp(ret_out, diff_out, xf, w["w_out"], w["norm_mix_post"], w["norm_ffn_pre"],
                 w["norm_ffn_post"], w["w_up"], w["w_down"], tm)
    return (y.reshape(batch, seq, D_MODEL), k_new, dv.reshape(batch, seq, DIFF_HEADS, HEAD), new_state)


def kernel(x_prompt, x_sample, cache_diff_k, cache_diff_v, state_ret, norm_mix_pre, norm_mix_post, w_in, diff_subln, lambda_q1, lambda_k1, lambda_q2, lambda_k2, w_out, norm_ffn_pre, norm_ffn_post, w_up, w_down):
    depth = w_in.shape[0]
    past = cache_diff_k.shape[2]
    yp, ys = x_prompt, x_sample
    outs = [[] for _ in range(6)]
    for l in range(depth):
        lambda_init = 0.8 - 0.6 * math.exp(-0.3 * l)
        w = dict(
            norm_mix_pre=norm_mix_pre[l][None], norm_mix_post=norm_mix_post[l][None],
            w_in=w_in[l].astype(BF16), diff_subln=diff_subln[l][None],
            lambda_q1=lambda_q1[l][None], lambda_k1=lambda_k1[l][None],
            lambda_q2=lambda_q2[l][None], lambda_k2=lambda_k2[l][None],
            w_out=w_out[l].astype(BF16), norm_ffn_pre=norm_ffn_pre[l][None],
            norm_ffn_post=norm_ffn_post[l][None],
            w_up=w_up[l].astype(BF16), w_down=w_down[l].astype(BF16))
        yp, kp, vp, sp = _layer(yp, 0, None, None, None, w, lambda_init,
                                tm_proj=TM_PROJ, tm=TM_MLP, ret_blk=RET_BLK)
        n_sample = x_sample.shape[0] * x_sample.shape[1]
        ys, ks, vs, ss = _layer(ys, past, state_ret[l], cache_diff_k[l], cache_diff_v[l], w, lambda_init,
                                tm_proj=n_sample, tm=n_sample, ret_blk=x_sample.shape[1])
        for lst, val in zip(outs, (kp, vp, sp, ks, vs, ss)):
            lst.append(val)
    return (yp, ys) + tuple(jnp.stack(o) for o in outs)
```

```python
import functools
import math

import jax
import jax.numpy as jnp
from jax import lax
from jax.experimental import pallas as pl
from jax.experimental.pallas import tpu as pltpu

D_MODEL = 1024
CHUNK = 64
HEAD = 128
RET_HEADS = 4
DIFF_HEADS = 4
DIFF_DK = 64
SEG = 512
N_SEG = 7
D_IN = N_SEG * SEG
D_FF = 4 * D_MODEL
ROPE_THETA = 10000.0
NORM_EPS = 1e-6
SUBLN_EPS = 1e-5
NEG_INF = -1e30
TM_PROJ = 1024
TM_MLP = 1024
MLP_ROW_PARTS = 4
MLP_MIN_PART_ROWS = 128
ATTN_BLK = 256
RET_BLK = 256
RET_BLOCKS_PER_ITER = 8
Q_SCALE = DIFF_DK ** -0.5 * math.log2(math.e)
ONES_ROWS = 16
VAUG = HEAD + ONES_ROWS
Q_BLOCKS_PER_REGION = 2
SKEW = 2

VMEM_LIMIT_BYTES = 56 * 1024 * 1024

F32 = jnp.float32
BF16 = jnp.bfloat16

_NT = (((1,), (1,)), ((), ()))
_TN = (((0,), (0,)), ((), ()))


def _params(*sem):
    return pltpu.CompilerParams(dimension_semantics=sem, vmem_limit_bytes=VMEM_LIMIT_BYTES)


def _const_spec(shape):
    return pl.BlockSpec(shape, lambda *_: (0,) * len(shape), pipeline_mode=pl.Buffered(1))


def _rope_table_kernel(c128_ref, s128_ref, c64_ref, s64_ref, *, seq, pos0):
    n = c128_ref.shape[0]
    lane = lax.broadcasted_iota(jnp.int32, (n, HEAD), 1)
    pos = (lax.broadcasted_iota(jnp.int32, (n, HEAD), 0) % seq + pos0).astype(F32)
    log_theta = math.log(ROPE_THETA)

    def table(d):
        half = d // 2
        r = lane % d
        i = (r % half).astype(F32)
        inv = jnp.exp(i * (-2.0 * log_theta / d))
        ang = pos * inv
        sign = jnp.where(r < half, -1.0, 1.0).astype(F32)
        return jnp.cos(ang), jnp.sin(ang) * sign

    c128_ref[...], s128_ref[...] = table(HEAD)
    c64_ref[...], s64_ref[...] = table(DIFF_DK)


def _rope_tables(n_rows, seq, pos0):
    shp = jax.ShapeDtypeStruct((n_rows, HEAD), F32)
    return pl.pallas_call(
        functools.partial(_rope_table_kernel, seq=seq, pos0=pos0),
        out_shape=(shp,) * 4,
        name="rope_tables",
    )()


def _proj_kernel(x_ref, nw_ref, w_ref, c128_ref, s128_ref, c64_ref, s64_ref,
                 rq_ref, rk_ref, rv_ref, rg_ref, dq_ref, dk_ref, dv_ref, dkb_ref, dvb_ref, *, transposed):
    tm = x_ref.shape[0]
    half_rows = tm // 2

    def norm_rows(rows):
        x = x_ref[rows, :]
        ms = jnp.mean(x * x, axis=-1, keepdims=True)
        return (x * lax.rsqrt(ms + NORM_EPS) * nw_ref[...]).astype(BF16)

    h_lo, h_hi = norm_rows(slice(0, half_rows)), norm_rows(slice(half_rows, tm))
    h = jnp.concatenate([h_lo, h_hi], axis=0)

    def seg(s, split=False):
        w = w_ref[:, s * SEG:(s + 1) * SEG]
        if split:
            return jnp.concatenate([jnp.dot(h_lo, w, preferred_element_type=F32),
                                    jnp.dot(h_hi, w, preferred_element_type=F32)], axis=0)
        return jnp.dot(h, w, preferred_element_type=F32)

    c128, s128 = c128_ref[...], s128_ref[...]
    c64, s64 = c64_ref[...], s64_ref[...]
    lane = lax.broadcasted_iota(jnp.int32, (tm, HEAD), 1)
    first_half64 = (lane % DIFF_DK) < (DIFF_DK // 2)

    def rope128(z):
        return z * c128 + pltpu.roll(z, HEAD // 2, 1) * s128

    def rope64(z):
        rot = jnp.where(first_half64, pltpu.roll(z, HEAD - DIFF_DK // 2, 1),
                        pltpu.roll(z, DIFF_DK // 2, 1))
        return z * c64 + rot * s64

    def store_blocks_t(ref, sl, zt, width):
        for blk in range(tm // width):
            ref[blk, sl, :] = zt[:, blk * width:(blk + 1) * width].astype(BF16)

    z = seg(6, split=True)
    for hd in range(DIFF_HEADS):
        dv_ref[pl.ds(hd, tm, stride=DIFF_HEADS), :] = z[:, hd * HEAD:(hd + 1) * HEAD]
    if transposed:
        for hd in range(SEG // HEAD):
            store_blocks_t(dvb_ref, slice(hd * VAUG, hd * VAUG + HEAD), z[:, hd * HEAD:(hd + 1) * HEAD].T,
                           ATTN_BLK)
            dvb_ref[:, hd * VAUG + HEAD:(hd + 1) * VAUG, :] = jnp.ones(
                (tm // ATTN_BLK, ONES_ROWS, ATTN_BLK), BF16)
    else:
        dvb_ref[...] = z.astype(BF16)
    z = seg(5)
    for hd in range(SEG // HEAD):
        sl = slice(hd * HEAD, (hd + 1) * HEAD)
        kr = rope64(z[:, sl])
        dkb_ref[:, sl] = kr.astype(BF16)
        if transposed:
            dk_ref[0, sl, :] = kr.T
        else:
            dk_ref[:, sl] = kr
    z = seg(4)
    for hd in range(SEG // HEAD):
        sl = slice(hd * HEAD, (hd + 1) * HEAD)
        qr = rope64(z[:, sl]) * Q_SCALE
        if transposed:
            store_blocks_t(dq_ref, sl, qr.T, ATTN_BLK)
        else:
            dq_ref[:, sl] = qr.astype(BF16)
    z = seg(1)
    for hd in range(SEG // HEAD):
        sl = slice(hd * HEAD, (hd + 1) * HEAD)
        kr = rope128(z[:, sl]) * (HEAD ** -0.5)
        if transposed:
            store_blocks_t(rk_ref, sl, kr.T, RET_BLK)
        else:
            rk_ref[:, sl] = kr.astype(BF16)
    z = seg(0)
    for hd in range(SEG // HEAD):
        sl = slice(hd * HEAD, (hd + 1) * HEAD)
        rq_ref[:, sl] = rope128(z[:, sl]).astype(BF16)
    rv_ref[...] = seg(2).astype(BF16)
    rg_ref[...] = seg(3).astype(BF16)


def _project(x, norm_w, w_in, tables, tm, seq, transposed):
    t = x.shape[0]
    nt = tables[0].shape[0] // tm
    row = pl.BlockSpec((tm, D_MODEL), lambda i: (i, 0))
    out = pl.BlockSpec((tm, SEG), lambda i: (i, 0))
    tab = pl.BlockSpec((tm, HEAD), lambda i: (i % nt, 0))
    b16 = jax.ShapeDtypeStruct((t, SEG), BF16)
    f32 = jax.ShapeDtypeStruct((t, SEG), F32)
    dv_spec = pl.BlockSpec((tm * DIFF_HEADS, HEAD), lambda i: (i, 0))
    dv_shape = jax.ShapeDtypeStruct((t * DIFF_HEADS, HEAD), F32)
    if transposed:
        per_seq = seq // tm
        def blk_t(rows, width):
            return (pl.BlockSpec((tm // width, rows, width), lambda i: (i, 0, 0)),
                    jax.ShapeDtypeStruct((t // width, rows, width), BF16))
        (rk_spec, rk_shape), (q_spec, q_shape) = blk_t(SEG, RET_BLK), blk_t(SEG, ATTN_BLK)
        v_spec, v_shape = blk_t(DIFF_HEADS * VAUG, ATTN_BLK)
        dk_t = pl.BlockSpec((1, SEG, tm), lambda i: (i // per_seq, 0, i % per_seq))
        out_specs = [out, rk_spec, out, out, q_spec, dk_t, dv_spec, out, v_spec]
        out_shape = [b16, rk_shape, b16, b16, q_shape, jax.ShapeDtypeStruct((t // seq, SEG, seq), F32),
                     dv_shape, b16, v_shape]
    else:
        out_specs = [out] * 6 + [dv_spec, out, out]
        out_shape = [b16, b16, b16, b16, b16, f32, dv_shape, b16, b16]
    return pl.pallas_call(
        functools.partial(_proj_kernel, transposed=transposed),
        grid=(t // tm,),
        in_specs=[row, _const_spec((1, D_MODEL)), _const_spec((D_MODEL, D_IN)), tab, tab, tab, tab],
        out_specs=out_specs,
        out_shape=out_shape,
        compiler_params=_params("parallel"),
        name="in_proj",
    )(x, norm_w, w_in, *tables)


def _log_gamma(h):
    return math.log1p(-(2.0 ** (-5.0 - h)))


def _ret_tables(dmat, qdec, kdec, blk, k_feature_major):
    rel = (lax.broadcasted_iota(jnp.int32, (blk, blk), 0)
           - lax.broadcasted_iota(jnp.int32, (blk, blk), 1))
    relf = jnp.maximum(rel, 0).astype(F32)
    row = lax.broadcasted_iota(jnp.int32, qdec.shape[1:], 0).astype(F32)
    along = lax.broadcasted_iota(jnp.int32, kdec.shape[1:], 1 if k_feature_major else 0).astype(F32)
    for h in range(RET_HEADS):
        lg = _log_gamma(h)
        dmat[h] = jnp.where(rel >= 0, jnp.exp(lg * relf), 0.0)
        qdec[h] = jnp.exp(lg * (row + 1.0))
        kdec[h] = jnp.exp(lg * (blk - 1.0 - along))


def _ret_head(idx, h, q_ref, k_ref, v_ref, g_ref, o_ref, state, dmat, qdec, kdec, blk, k_feature_major):
    rows = pl.ds(pl.multiple_of(idx * blk, blk), blk)
    sl = slice(h * HEAD, (h + 1) * HEAD)
    q, v = q_ref[rows, sl], v_ref[rows, sl]
    st = state[h]
    if k_feature_major:
        kt = k_ref[idx, sl, :]
        s = jnp.dot(q, kt, preferred_element_type=F32)
        kd = (kt.astype(F32) * kdec[h]).astype(BF16)
        upd = jnp.dot(kd, v, preferred_element_type=F32)
    else:
        k = k_ref[rows, sl]
        s = lax.dot_general(q, k, _NT, preferred_element_type=F32)
        kd = (k.astype(F32) * kdec[h]).astype(BF16)
        upd = lax.dot_general(kd, v, _TN, preferred_element_type=F32)
    o = jnp.dot((s * dmat[h]).astype(BF16), v, preferred_element_type=F32)
    o = o + jnp.dot(q, st.astype(BF16), preferred_element_type=F32) * qdec[h]
    state[h] = math.exp(_log_gamma(h) * blk) * st + upd
    ro = o * lax.rsqrt(jnp.mean(o * o, axis=-1, keepdims=True) + NORM_EPS)
    half_g = 0.5 * g_ref[rows, sl].astype(F32)
    o_ref[rows, sl] = ((half_g + half_g * jnp.tanh(half_g)) * ro).astype(BF16)


def _ret_kernel(*refs, blk, has_init, k_feature_major, blocks_per_iter):
    if has_init:
        q_ref, k_ref, v_ref, g_ref, s0_ref, o_ref, sout_ref, state, dmat, qdec, kdec = refs
    else:
        q_ref, k_ref, v_ref, g_ref, o_ref, sout_ref, state, dmat, qdec, kdec = refs

    @pl.when(pl.program_id(0) == 0)
    def _():
        _ret_tables(dmat, qdec, kdec, blk, k_feature_major)

    if has_init:
        state[...] = s0_ref[0]
    else:
        state[...] = jnp.zeros_like(state)

    def group(g, carry):
        for sub in range(blocks_per_iter):
            for h in range(RET_HEADS):
                _ret_head(g * blocks_per_iter + sub, h, q_ref, k_ref, v_ref, g_ref, o_ref,
                          state, dmat, qdec, kdec, blk, k_feature_major)
        return carry

    lax.fori_loop(0, q_ref.shape[0] // (blk * blocks_per_iter), group, 0)
    sout_ref[0] = state[...]


def _retention(rq, rk, rv, rg, state0, batch, seq, blk, blocks_per_iter):
    tok = pl.BlockSpec((seq, SEG), lambda b: (b, 0))
    st_spec = pl.BlockSpec((1, RET_HEADS, HEAD, HEAD), lambda b: (b, 0, 0, 0))
    has_init = state0 is not None
    k_feature_major = rk.ndim == 3
    k_spec = pl.BlockSpec((seq // blk, SEG, blk), lambda b: (b, 0, 0)) if k_feature_major else tok
    args = (rq, rk, rv, rg) + ((state0,) if has_init else ())
    return pl.pallas_call(
        functools.partial(_ret_kernel, blk=blk, has_init=has_init, k_feature_major=k_feature_major,
                          blocks_per_iter=blocks_per_iter),
        grid=(batch,),
        in_specs=[tok, k_spec, tok, tok] + ([st_spec] if has_init else []),
        out_specs=[tok, st_spec],
        out_shape=[jax.ShapeDtypeStruct((batch * seq, SEG), BF16),
                   jax.ShapeDtypeStruct((batch, RET_HEADS, HEAD, HEAD), F32)],
        scratch_shapes=[pltpu.VMEM((RET_HEADS, HEAD, HEAD), F32),
                        pltpu.VMEM((RET_HEADS, blk, blk), F32),
                        pltpu.VMEM((RET_HEADS, blk, HEAD), F32),
                        pltpu.VMEM((RET_HEADS, 1, blk) if k_feature_major else (RET_HEADS, blk, HEAD), F32)],
        compiler_params=_params("arbitrary"),
        name="retention",
    )(*args)


def _lambda(lq1_ref, lk1_ref, lq2_ref, lk2_ref, lambda_init):
    a = jnp.sum(lq1_ref[...] * lk1_ref[...], axis=-1, keepdims=True)
    b = jnp.sum(lq2_ref[...] * lk2_ref[...], axis=-1, keepdims=True)
    return jnp.exp(a) - jnp.exp(b) + lambda_init


def _stack_heads(q):
    lane = lax.broadcasted_iota(jnp.int32, q.shape, 1)
    zero = jnp.zeros_like(q)
    return jnp.concatenate([jnp.where(lane < DIFF_DK, q, zero),
                            jnp.where(lane >= DIFF_DK, q, zero)], axis=0)


def _diff_finish(acc, l, n, lam, sub, lambda_init):
    o = acc[:n] * (1.0 / l[:n]) - lam * (acc[n:] * (1.0 / l[n:]))
    o = o * lax.rsqrt(jnp.mean(o * o, axis=-1, keepdims=True) + SUBLN_EPS)
    return o * sub * (1.0 - lambda_init)


def _attn_kernel(lq1_ref, lk1_ref, lq2_ref, lk2_ref, sub_ref, qt_ref, k_ref, vt_ref, o_ref,
                 qs_ref, acc_ref, *, lambda_init):
    lam = _lambda(lq1_ref, lk1_ref, lq2_ref, lk2_ref, lambda_init)
    out_gain = sub_ref[...] * (1.0 - lambda_init)
    n_regions = qt_ref.shape[0] // Q_BLOCKS_PER_REGION

    def region(r, carry):
        for static_r in range(n_regions):
            @pl.when(r == static_r)
            def _(static_r=static_r):
                first = static_r * Q_BLOCKS_PER_REGION
                _attn_region(range(first, first + Q_BLOCKS_PER_REGION), lam, out_gain,
                             qt_ref, k_ref, vt_ref, o_ref, qs_ref, acc_ref)
        return carry

    lax.fori_loop(0, n_regions, region, 0)


def _attn_region(q_blocks, lam, out_gain, qt_ref, k_ref, vt_ref, o_ref, qs_ref, acc_ref):
    blk = ATTN_BLK
    row = lax.broadcasted_iota(jnp.int32, (HEAD, blk), 0)
    key = lax.broadcasted_iota(jnp.int32, (blk, 2 * blk), 0)
    qry = lax.broadcasted_iota(jnp.int32, (blk, 2 * blk), 1) % blk
    visible = key // CHUNK <= qry // CHUNK
    items = [(qi, j, h) for qi in q_blocks for j in range(qi + 1) for h in range(DIFF_HEADS)]
    m_cur = {}

    def setup(qi):
        slot = qi % 2
        for h in range(DIFF_HEADS):
            qt = qt_ref[qi, h * HEAD:(h + 1) * HEAD, :]
            zero = jnp.zeros_like(qt)
            qs_ref[slot, h, :, :blk] = jnp.where(row < DIFF_DK, qt, zero)
            qs_ref[slot, h, :, blk:] = jnp.where(row >= DIFF_DK, qt, zero)
            m_cur[qi, h] = jnp.full((1, 2 * blk), NEG_INF, F32)
        acc_ref[slot] = jnp.zeros(acc_ref.shape[1:], F32)

    def scores(qi, j, h):
        if (qi, h) not in m_cur:
            setup(qi)
        s = jnp.dot(k_ref[j * blk:(j + 1) * blk, h * HEAD:(h + 1) * HEAD], qs_ref[qi % 2, h],
                    preferred_element_type=F32)
        if j == qi:
            s = jnp.where(visible, s, NEG_INF)
        m_prev = m_cur[qi, h]
        m_cur[qi, h] = jnp.maximum(m_prev, jnp.max(s, axis=0, keepdims=True))
        return qi, j, h, s, m_prev, m_cur[qi, h]

    def accumulate(qi, j, h, s, m_prev, m_new):
        alpha = jnp.exp2(m_prev - m_new)
        p = jnp.exp2(s - m_new).astype(BF16)
        pv = jnp.dot(vt_ref[j, h * VAUG:(h + 1) * VAUG, :], p, preferred_element_type=F32)
        acc_ref[qi % 2, h] = alpha * acc_ref[qi % 2, h] + pv

    def finish(qi, h):
        acc = acc_ref[qi % 2, h]
        o = acc[:HEAD, :] * (1.0 / acc[HEAD:HEAD + 1, :])
        o = o[:, :blk] - lam * o[:, blk:]
        o = o * lax.rsqrt(jnp.mean(o * o, axis=0, keepdims=True) + SUBLN_EPS) * out_gain
        o_ref[qi * blk:(qi + 1) * blk, h * HEAD:(h + 1) * HEAD] = o.T.astype(BF16)

    pending = [scores(*it) for it in items[:SKEW]]
    for i in range(len(items)):
        current = pending.pop(0)
        if i + SKEW < len(items):
            pending.append(scores(*items[i + SKEW]))
        accumulate(*current)
        qi, j, h = current[:3]
        if j == qi:
            finish(qi, h)


def _diff_attention_prompt(lams, subln_col, dqt, dkb, dvt, batch, seq, lambda_init):
    blk = ATTN_BLK
    nq = seq // blk
    vec = _const_spec((1, DIFF_DK))
    return pl.pallas_call(
        functools.partial(_attn_kernel, lambda_init=lambda_init),
        grid=(batch,),
        in_specs=[vec] * 4 + [_const_spec((HEAD, 1)),
                  pl.BlockSpec((nq, SEG, blk), lambda b: (b, 0, 0)),
                  pl.BlockSpec((seq, SEG), lambda b: (b, 0)),
                  pl.BlockSpec((nq, DIFF_HEADS * VAUG, blk), lambda b: (b, 0, 0))],
        out_specs=pl.BlockSpec((seq, SEG), lambda b: (b, 0)),
        out_shape=jax.ShapeDtypeStruct((batch * seq, SEG), BF16),
        scratch_shapes=[pltpu.VMEM((2, DIFF_HEADS, HEAD, 2 * blk), BF16),
                        pltpu.VMEM((2, DIFF_HEADS, VAUG, 2 * blk), F32)],
        compiler_params=_params("parallel"),
        name="diff_attn_prompt",
    )(*lams, subln_col, dqt, dkb, dvt)


def _attn_sample_kernel(lq1_ref, lk1_ref, lq2_ref, lk2_ref, sub_ref, q_ref, kpt_ref, vp_ref,
                        kn_ref, vn_ref, o_ref, *, lambda_init):
    n = q_ref.shape[0]
    past = kpt_ref.shape[2]
    lam = _lambda(lq1_ref, lk1_ref, lq2_ref, lk2_ref, lambda_init)
    for h in range(DIFF_HEADS):
        sl = slice(h * HEAD, (h + 1) * HEAD)
        qs = _stack_heads(q_ref[:, sl])
        s_p = jnp.dot(qs, kpt_ref[0, sl, :].astype(BF16), preferred_element_type=F32)
        s_n = lax.dot_general(qs, kn_ref[:, sl], _NT, preferred_element_type=F32)
        m = jnp.maximum(jnp.max(s_p, axis=-1, keepdims=True), jnp.max(s_n, axis=-1, keepdims=True))
        p_p = jnp.exp2(s_p - m)
        p_n = jnp.exp2(s_n - m)
        l = jnp.sum(p_p, axis=-1, keepdims=True) + jnp.sum(p_n, axis=-1, keepdims=True)
        v_p = vp_ref[0, pl.ds(h, past, stride=DIFF_HEADS), :].astype(BF16)
        acc = (jnp.dot(p_p.astype(BF16), v_p, preferred_element_type=F32)
               + jnp.dot(p_n.astype(BF16), vn_ref[:, sl], preferred_element_type=F32))
        o_ref[:, sl] = _diff_finish(acc, l, n, lam, sub_ref[...], lambda_init).astype(BF16)


def _diff_attention_sample(lams, subln, dq, kt_past, v_past, dkb, dvb, batch, seq, lambda_init):
    past = kt_past.shape[2]
    vec = _const_spec((1, DIFF_DK))
    new = pl.BlockSpec((seq, SEG), lambda b: (b, 0))
    return pl.pallas_call(
        functools.partial(_attn_sample_kernel, lambda_init=lambda_init),
        grid=(batch,),
        in_specs=[vec] * 4 + [_const_spec((1, HEAD)), new,
                  pl.BlockSpec((1, SEG, past), lambda b: (b, 0, 0)),
                  pl.BlockSpec((1, DIFF_HEADS * past, HEAD), lambda b: (b, 0, 0)), new, new],
        out_specs=new,
        out_shape=jax.ShapeDtypeStruct((batch * seq, SEG), BF16),
        compiler_params=_params("parallel"),
        name="diff_attn_sample",
    )(*lams, subln, dq, kt_past, v_past, dkb, dvb)


def _rms(x, w):
    return x * lax.rsqrt(jnp.mean(x * x, axis=-1, keepdims=True) + NORM_EPS) * w


def _mlp_kernel(ret_ref, dif_ref, x_ref, wo_ref, n1_ref, n2_ref, n3_ref, wu_ref, wd_ref, y_ref, u_ref,
                *, ff_blk):
    tm = x_ref.shape[0]
    n_parts = max(1, min(MLP_ROW_PARTS, tm // MLP_MIN_PART_ROWS))
    part = tm // n_parts
    parts = [slice(i * part, (i + 1) * part) for i in range(n_parts)]

    def front(rows):
        mix = (jnp.dot(ret_ref[rows, :], wo_ref[:SEG, :], preferred_element_type=F32)
               + jnp.dot(dif_ref[rows, :], wo_ref[SEG:, :], preferred_element_type=F32))
        x1 = x_ref[rows, :] + _rms(mix, n1_ref[...])
        return x1, _rms(x1, n2_ref[...]).astype(BF16)

    def up(h, rows, c):
        sl = slice(c * ff_blk, (c + 1) * ff_blk)
        u = jnp.maximum(jnp.dot(h, wu_ref[:, sl], preferred_element_type=F32), 0.0)
        u_ref[rows, sl] = (u * u).astype(BF16)

    fronts = [front(rows) for rows in parts]
    for rows, (_, h_part) in zip(parts, fronts):
        up(h_part, rows, 0)
    h = jnp.concatenate([h_part for _, h_part in fronts], axis=0)
    for c in range(1, D_FF // ff_blk):
        up(h, slice(None), c)
    for rows, (x1, _) in zip(parts, fronts):
        f = jnp.dot(u_ref[rows, :], wd_ref[...], preferred_element_type=F32)
        y_ref[rows, :] = x1 + _rms(f, n3_ref[...])


def _out_mlp(ret_out, diff_out, x, w_out, n1, n2, n3, w_up, w_down, tm, ff_blk=1024):
    t = x.shape[0]
    half = pl.BlockSpec((tm, SEG), lambda i: (i, 0))
    row = pl.BlockSpec((tm, D_MODEL), lambda i: (i, 0))
    nrm = _const_spec((1, D_MODEL))
    return pl.pallas_call(
        functools.partial(_mlp_kernel, ff_blk=ff_blk),
        grid=(t // tm,),
        in_specs=[half, half, row, _const_spec((D_MODEL, D_MODEL)), nrm, nrm, nrm,
                  _const_spec((D_MODEL, D_FF)), _const_spec((D_FF, D_MODEL))],
        out_specs=row,
        out_shape=jax.ShapeDtypeStruct((t, D_MODEL), F32),
        scratch_shapes=[pltpu.VMEM((tm, D_FF), BF16)],
        compiler_params=_params("parallel"),
        name="out_mlp",
    )(ret_out, diff_out, x, w_out, n1, n2, n3, w_up, w_down)


def _layer(x, pos0, state0, k_past, v_past, w, lambda_init, tm_proj, tm, ret_blk):
    batch, seq, _ = x.shape
    prompt = k_past is None
    xf = x.reshape(batch * seq, D_MODEL)
    tables = _rope_tables(max(seq, tm_proj), seq, pos0)
    rq, rk, rv, rg, dq, dk, dv, dkb, dvb = _project(xf, w["norm_mix_pre"], w["w_in"], tables, tm_proj, seq,
                                                    transposed=prompt)
    ret_out, new_state = _retention(rq, rk, rv, rg, state0, batch, seq, ret_blk,
                                    blocks_per_iter=RET_BLOCKS_PER_ITER if prompt else 1)
    lams = (w["lambda_q1"], w["lambda_k1"], w["lambda_q2"], w["lambda_k2"])
    if prompt:
        diff_out = _diff_attention_prompt(lams, w["diff_subln"].reshape(HEAD, 1), dq, dkb, dvb,
                                          batch, seq, lambda_init)
        k_new = dk.reshape(batch, 2 * DIFF_HEADS, DIFF_DK, seq).transpose(0, 3, 1, 2)
    else:
        past = k_past.shape[1]
        kt_past = k_past.transpose(0, 2, 3, 1).reshape(batch, SEG, past)
        diff_out = _diff_attention_sample(
            lams, w["diff_subln"], dq, kt_past, v_past.reshape(batch, past * DIFF_HEADS, HEAD),
            dkb, dvb, batch, seq, lambda_init)
        k_new = dk.reshape(batch, seq, 2 * DIFF_HEADS, DIFF_DK)
    y = _out_mlp(ret_out, diff_out, xf, w["w_out"], w["norm_mix_post"], w["norm_ffn_pre"],
                 w["norm_ffn_post"], w["w_up"], w["w_down"], tm)
    return (y.reshape(batch, seq, D_MODEL), k_new, dv.reshape(batch, seq, DIFF_HEADS, HEAD), new_state)


def kernel(x_prompt, x_sample, cache_diff_k, cache_diff_v, state_ret, norm_mix_pre, norm_mix_post, w_in, diff_subln, lambda_q1, lambda_k1, lambda_q2, lambda_k2, w_out, norm_ffn_pre, norm_ffn_post, w_up, w_down):
    depth = w_in.shape[0]
    past = cache_diff_k.shape[2]
    yp, ys = x_prompt, x_sample
    outs = [[] for _ in range(6)]
    for l in range(depth):
        lambda_init = 0.8 - 0.6 * math.exp(-0.3 * l)
        w = dict(
            norm_mix_pre=norm_mix_pre[l][None], norm_mix_post=norm_mix_post[l][None],
            w_in=w_in[l].astype(BF16), diff_subln=diff_subln[l][None],
            lambda_q1=lambda_q1[l][None], lambda_k1=lambda_k1[l][None],
            lambda_q2=lambda_q2[l][None], lambda_k2=lambda_k2[l][None],
            w_out=w_out[l].astype(BF16), norm_ffn_pre=norm_ffn_pre[l][None],
            norm_ffn_post=norm_ffn_post[l][None],
            w_up=w_up[l].astype(BF16), w_down=w_down[l].astype(BF16))
        yp, kp, vp, sp = _layer(yp, 0, None, None, None, w, lambda_init,
                                tm_proj=TM_PROJ, tm=TM_MLP, ret_blk=RET_BLK)
        n_sample = x_sample.shape[0] * x_sample.shape[1]
        ys, ks, vs, ss = _layer(ys, past, state_ret[l], cache_diff_k[l], cache_diff_v[l], w, lambda_init,
                                tm_proj=n_sample, tm=n_sample, ret_blk=x_sample.shape[1])
        for lst, val in zip(outs, (kp, vp, sp, ks, vs, ss)):
            lst.append(val)
    return (yp, ys) + tuple(jnp.stack(o) for o in outs)
```

```python
import functools
import math

import jax
import jax.numpy as jnp
from jax import lax
from jax.experimental import pallas as pl
from jax.experimental.pallas import tpu as pltpu

D_MODEL = 1024
CHUNK = 64
HEAD = 128
RET_HEADS = 4
DIFF_HEADS = 4
DIFF_DK = 64
SEG = 512
N_SEG = 7
D_IN = N_SEG * SEG
D_FF = 4 * D_MODEL
ROPE_THETA = 10000.0
NORM_EPS = 1e-6
SUBLN_EPS = 1e-5
NEG_INF = -1e30
TM_PROJ = 1024
TM_MLP = 1024
MLP_ROW_PARTS = 4
MLP_MIN_PART_ROWS = 128
ATTN_BLK = 256
RET_BLK = 256
RET_BLOCKS_PER_ITER = 8
Q_SCALE = DIFF_DK ** -0.5 * math.log2(math.e)
ONES_ROWS = 16
VAUG = HEAD + ONES_ROWS
Q_BLOCKS_PER_REGION = 4
SKEW = 2

VMEM_LIMIT_BYTES = 56 * 1024 * 1024

F32 = jnp.float32
BF16 = jnp.bfloat16

_NT = (((1,), (1,)), ((), ()))
_TN = (((0,), (0,)), ((), ()))


def _params(*sem):
    return pltpu.CompilerParams(dimension_semantics=sem, vmem_limit_bytes=VMEM_LIMIT_BYTES)


def _const_spec(shape):
    return pl.BlockSpec(shape, lambda *_: (0,) * len(shape), pipeline_mode=pl.Buffered(1))


def _rope_table_kernel(c128_ref, s128_ref, c64_ref, s64_ref, *, seq, pos0):
    n = c128_ref.shape[0]
    lane = lax.broadcasted_iota(jnp.int32, (n, HEAD), 1)
    pos = (lax.broadcasted_iota(jnp.int32, (n, HEAD), 0) % seq + pos0).astype(F32)
    log_theta = math.log(ROPE_THETA)

    def table(d):
        half = d // 2
        r = lane % d
        i = (r % half).astype(F32)
        inv = jnp.exp(i * (-2.0 * log_theta / d))
        ang = pos * inv
        sign = jnp.where(r < half, -1.0, 1.0).astype(F32)
        return jnp.cos(ang), jnp.sin(ang) * sign

    c128_ref[...], s128_ref[...] = table(HEAD)
    c64_ref[...], s64_ref[...] = table(DIFF_DK)


def _rope_tables(n_rows, seq, pos0):
    shp = jax.ShapeDtypeStruct((n_rows, HEAD), F32)
    return pl.pallas_call(
        functools.partial(_rope_table_kernel, seq=seq, pos0=pos0),
        out_shape=(shp,) * 4,
        name="rope_tables",
    )()


def _proj_kernel(x_ref, nw_ref, w_ref, c128_ref, s128_ref, c64_ref, s64_ref,
                 rq_ref, rk_ref, rv_ref, rg_ref, dq_ref, dk_ref, dv_ref, dkb_ref, dvb_ref, *, transposed):
    tm = x_ref.shape[0]
    half_rows = tm // 2

    def norm_rows(rows):
        x = x_ref[rows, :]
        ms = jnp.mean(x * x, axis=-1, keepdims=True)
        return (x * lax.rsqrt(ms + NORM_EPS) * nw_ref[...]).astype(BF16)

    h_lo, h_hi = norm_rows(slice(0, half_rows)), norm_rows(slice(half_rows, tm))
    h = jnp.concatenate([h_lo, h_hi], axis=0)

    def seg(s, split=False):
        w = w_ref[:, s * SEG:(s + 1) * SEG]
        if split:
            return jnp.concatenate([jnp.dot(h_lo, w, preferred_element_type=F32),
                                    jnp.dot(h_hi, w, preferred_element_type=F32)], axis=0)
        return jnp.dot(h, w, preferred_element_type=F32)

    c128, s128 = c128_ref[...], s128_ref[...]
    c64, s64 = c64_ref[...], s64_ref[...]
    lane = lax.broadcasted_iota(jnp.int32, (tm, HEAD), 1)
    first_half64 = (lane % DIFF_DK) < (DIFF_DK // 2)

    def rope128(z):
        return z * c128 + pltpu.roll(z, HEAD // 2, 1) * s128

    def rope64(z):
        rot = jnp.where(first_half64, pltpu.roll(z, HEAD - DIFF_DK // 2, 1),
                        pltpu.roll(z, DIFF_DK // 2, 1))
        return z * c64 + rot * s64

    def store_blocks_t(ref, sl, zt, width):
        for blk in range(tm // width):
            ref[blk, sl, :] = zt[:, blk * width:(blk + 1) * width].astype(BF16)

    z = seg(6, split=True)
    for hd in range(DIFF_HEADS):
        dv_ref[pl.ds(hd, tm, stride=DIFF_HEADS), :] = z[:, hd * HEAD:(hd + 1) * HEAD]
    if transposed:
        for hd in range(SEG // HEAD):
            store_blocks_t(dvb_ref, slice(hd * VAUG, hd * VAUG + HEAD), z[:, hd * HEAD:(hd + 1) * HEAD].T,
                           ATTN_BLK)
            dvb_ref[:, hd * VAUG + HEAD:(hd + 1) * VAUG, :] = jnp.ones(
                (tm // ATTN_BLK, ONES_ROWS, ATTN_BLK), BF16)
    else:
        dvb_ref[...] = z.astype(BF16)
    z = seg(5)
    for hd in range(SEG // HEAD):
        sl = slice(hd * HEAD, (hd + 1) * HEAD)
        kr = rope64(z[:, sl])
        dkb_ref[:, sl] = kr.astype(BF16)
        if transposed:
            dk_ref[0, sl, :] = kr.T
        else:
            dk_ref[:, sl] = kr
    z = seg(4)
    for hd in range(SEG // HEAD):
        sl = slice(hd * HEAD, (hd + 1) * HEAD)
        qr = rope64(z[:, sl]) * Q_SCALE
        if transposed:
            store_blocks_t(dq_ref, sl, qr.T, ATTN_BLK)
        else:
            dq_ref[:, sl] = qr.astype(BF16)
    z = seg(1)
    for hd in range(SEG // HEAD):
        sl = slice(hd * HEAD, (hd + 1) * HEAD)
        kr = rope128(z[:, sl]) * (HEAD ** -0.5)
        if transposed:
            store_blocks_t(rk_ref, sl, kr.T, RET_BLK)
        else:
            rk_ref[:, sl] = kr.astype(BF16)
    z = seg(0)
    for hd in range(SEG // HEAD):
        sl = slice(hd * HEAD, (hd + 1) * HEAD)
        rq_ref[:, sl] = rope128(z[:, sl]).astype(BF16)
    rv_ref[...] = seg(2).astype(BF16)
    rg_ref[...] = seg(3).astype(BF16)


def _project(x, norm_w, w_in, tables, tm, seq, transposed):
    t = x.shape[0]
    nt = tables[0].shape[0] // tm
    row = pl.BlockSpec((tm, D_MODEL), lambda i: (i, 0))
    out = pl.BlockSpec((tm, SEG), lambda i: (i, 0))
    tab = pl.BlockSpec((tm, HEAD), lambda i: (i % nt, 0))
    b16 = jax.ShapeDtypeStruct((t, SEG), BF16)
    f32 = jax.ShapeDtypeStruct((t, SEG), F32)
    dv_spec = pl.BlockSpec((tm * DIFF_HEADS, HEAD), lambda i: (i, 0))
    dv_shape = jax.ShapeDtypeStruct((t * DIFF_HEADS, HEAD), F32)
    if transposed:
        per_seq = seq // tm
        def blk_t(rows, width):
            return (pl.BlockSpec((tm // width, rows, width), lambda i: (i, 0, 0)),
                    jax.ShapeDtypeStruct((t // width, rows, width), BF16))
        (rk_spec, rk_shape), (q_spec, q_shape) = blk_t(SEG, RET_BLK), blk_t(SEG, ATTN_BLK)
        v_spec, v_shape = blk_t(DIFF_HEADS * VAUG, ATTN_BLK)
        dk_t = pl.BlockSpec((1, SEG, tm), lambda i: (i // per_seq, 0, i % per_seq))
        out_specs = [out, rk_spec, out, out, q_spec, dk_t, dv_spec, out, v_spec]
        out_shape = [b16, rk_shape, b16, b16, q_shape, jax.ShapeDtypeStruct((t // seq, SEG, seq), F32),
                     dv_shape, b16, v_shape]
    else:
        out_specs = [out] * 6 + [dv_spec, out, out]
        out_shape = [b16, b16, b16, b16, b16, f32, dv_shape, b16, b16]
    return pl.pallas_call(
        functools.partial(_proj_kernel, transposed=transposed),
        grid=(t // tm,),
        in_specs=[row, _const_spec((1, D_MODEL)), _const_spec((D_MODEL, D_IN)), tab, tab, tab, tab],
        out_specs=out_specs,
        out_shape=out_shape,
        compiler_params=_params("parallel"),
        name="in_proj",
    )(x, norm_w, w_in, *tables)


def _log_gamma(h):
    return math.log1p(-(2.0 ** (-5.0 - h)))


def _ret_tables(dmat, qdec, kdec, blk, k_feature_major):
    rel = (lax.broadcasted_iota(jnp.int32, (blk, blk), 0)
           - lax.broadcasted_iota(jnp.int32, (blk, blk), 1))
    relf = jnp.maximum(rel, 0).astype(F32)
    row = lax.broadcasted_iota(jnp.int32, qdec.shape[1:], 0).astype(F32)
    along = lax.broadcasted_iota(jnp.int32, kdec.shape[1:], 1 if k_feature_major else 0).astype(F32)
    for h in range(RET_HEADS):
        lg = _log_gamma(h)
        dmat[h] = jnp.where(rel >= 0, jnp.exp(lg * relf), 0.0)
        qdec[h] = jnp.exp(lg * (row + 1.0))
        kdec[h] = jnp.exp(lg * (blk - 1.0 - along))


def _ret_head(idx, h, q_ref, k_ref, v_ref, g_ref, o_ref, state, dmat, qdec, kdec, blk, k_feature_major):
    rows = pl.ds(pl.multiple_of(idx * blk, blk), blk)
    sl = slice(h * HEAD, (h + 1) * HEAD)
    q, v = q_ref[rows, sl], v_ref[rows, sl]
    st = state[h]
    if k_feature_major:
        kt = k_ref[idx, sl, :]
        s = jnp.dot(q, kt, preferred_element_type=F32)
        kd = (kt.astype(F32) * kdec[h]).astype(BF16)
        upd = jnp.dot(kd, v, preferred_element_type=F32)
    else:
        k = k_ref[rows, sl]
        s = lax.dot_general(q, k, _NT, preferred_element_type=F32)
        kd = (k.astype(F32) * kdec[h]).astype(BF16)
        upd = lax.dot_general(kd, v, _TN, preferred_element_type=F32)
    o = jnp.dot((s * dmat[h]).astype(BF16), v, preferred_element_type=F32)
    o = o + jnp.dot(q, st.astype(BF16), preferred_element_type=F32) * qdec[h]
    state[h] = math.exp(_log_gamma(h) * blk) * st + upd
    ro = o * lax.rsqrt(jnp.mean(o * o, axis=-1, keepdims=True) + NORM_EPS)
    half_g = 0.5 * g_ref[rows, sl].astype(F32)
    o_ref[rows, sl] = ((half_g + half_g * jnp.tanh(half_g)) * ro).astype(BF16)


def _ret_kernel(*refs, blk, has_init, k_feature_major, blocks_per_iter):
    if has_init:
        q_ref, k_ref, v_ref, g_ref, s0_ref, o_ref, sout_ref, state, dmat, qdec, kdec = refs
    else:
        q_ref, k_ref, v_ref, g_ref, o_ref, sout_ref, state, dmat, qdec, kdec = refs

    @pl.when(pl.program_id(0) == 0)
    def _():
        _ret_tables(dmat, qdec, kdec, blk, k_feature_major)

    if has_init:
        state[...] = s0_ref[0]
    else:
        state[...] = jnp.zeros_like(state)

    def group(g, carry):
        for sub in range(blocks_per_iter):
            for h in range(RET_HEADS):
                _ret_head(g * blocks_per_iter + sub, h, q_ref, k_ref, v_ref, g_ref, o_ref,
                          state, dmat, qdec, kdec, blk, k_feature_major)
        return carry

    lax.fori_loop(0, q_ref.shape[0] // (blk * blocks_per_iter), group, 0)
    sout_ref[0] = state[...]


def _retention(rq, rk, rv, rg, state0, batch, seq, blk, blocks_per_iter):
    tok = pl.BlockSpec((seq, SEG), lambda b: (b, 0))
    st_spec = pl.BlockSpec((1, RET_HEADS, HEAD, HEAD), lambda b: (b, 0, 0, 0))
    has_init = state0 is not None
    k_feature_major = rk.ndim == 3
    k_spec = pl.BlockSpec((seq // blk, SEG, blk), lambda b: (b, 0, 0)) if k_feature_major else tok
    args = (rq, rk, rv, rg) + ((state0,) if has_init else ())
    return pl.pallas_call(
        functools.partial(_ret_kernel, blk=blk, has_init=has_init, k_feature_major=k_feature_major,
                          blocks_per_iter=blocks_per_iter),
        grid=(batch,),
        in_specs=[tok, k_spec, tok, tok] + ([st_spec] if has_init else []),
        out_specs=[tok, st_spec],
        out_shape=[jax.ShapeDtypeStruct((batch * seq, SEG), BF16),
                   jax.ShapeDtypeStruct((batch, RET_HEADS, HEAD, HEAD), F32)],
        scratch_shapes=[pltpu.VMEM((RET_HEADS, HEAD, HEAD), F32),
                        pltpu.VMEM((RET_HEADS, blk, blk), F32),
                        pltpu.VMEM((RET_HEADS, blk, HEAD), F32),
                        pltpu.VMEM((RET_HEADS, 1, blk) if k_feature_major else (RET_HEADS, blk, HEAD), F32)],
        compiler_params=_params("arbitrary"),
        name="retention",
    )(*args)


def _lambda(lq1_ref, lk1_ref, lq2_ref, lk2_ref, lambda_init):
    a = jnp.sum(lq1_ref[...] * lk1_ref[...], axis=-1, keepdims=True)
    b = jnp.sum(lq2_ref[...] * lk2_ref[...], axis=-1, keepdims=True)
    return jnp.exp(a) - jnp.exp(b) + lambda_init


def _stack_heads(q):
    lane = lax.broadcasted_iota(jnp.int32, q.shape, 1)
    zero = jnp.zeros_like(q)
    return jnp.concatenate([jnp.where(lane < DIFF_DK, q, zero),
                            jnp.where(lane >= DIFF_DK, q, zero)], axis=0)


def _diff_finish(acc, l, n, lam, sub, lambda_init):
    o = acc[:n] * (1.0 / l[:n]) - lam * (acc[n:] * (1.0 / l[n:]))
    o = o * lax.rsqrt(jnp.mean(o * o, axis=-1, keepdims=True) + SUBLN_EPS)
    return o * sub * (1.0 - lambda_init)


def _attn_kernel(lq1_ref, lk1_ref, lq2_ref, lk2_ref, sub_ref, qt_ref, k_ref, vt_ref, o_ref,
                 qs_ref, acc_ref, *, lambda_init):
    lam = _lambda(lq1_ref, lk1_ref, lq2_ref, lk2_ref, lambda_init)
    out_gain = sub_ref[...] * (1.0 - lambda_init)
    n_regions = qt_ref.shape[0] // Q_BLOCKS_PER_REGION

    def region(r, carry):
        for static_r in range(n_regions):
            @pl.when(r == static_r)
            def _(static_r=static_r):
                first = static_r * Q_BLOCKS_PER_REGION
                _attn_region(range(first, first + Q_BLOCKS_PER_REGION), lam, out_gain,
                             qt_ref, k_ref, vt_ref, o_ref, qs_ref, acc_ref)
        return carry

    lax.fori_loop(0, n_regions, region, 0)


def _attn_region(q_blocks, lam, out_gain, qt_ref, k_ref, vt_ref, o_ref, qs_ref, acc_ref):
    assert len(q_blocks) <= qs_ref.shape[0]
    blk = ATTN_BLK
    row = lax.broadcasted_iota(jnp.int32, (HEAD, blk), 0)
    key = lax.broadcasted_iota(jnp.int32, (blk, 2 * blk), 0)
    qry = lax.broadcasted_iota(jnp.int32, (blk, 2 * blk), 1) % blk
    visible = key // CHUNK <= qry // CHUNK
    items = [(qi, j, h) for qi in q_blocks for j in range(qi + 1) for h in range(DIFF_HEADS)]
    m_cur = {}

    def setup(qi):
        slot = qi % Q_BLOCKS_PER_REGION
        for h in range(DIFF_HEADS):
            qt = qt_ref[qi, h * HEAD:(h + 1) * HEAD, :]
            zero = jnp.zeros_like(qt)
            qs_ref[slot, h, :, :blk] = jnp.where(row < DIFF_DK, qt, zero)
            qs_ref[slot, h, :, blk:] = jnp.where(row >= DIFF_DK, qt, zero)
            m_cur[qi, h] = jnp.full((1, 2 * blk), NEG_INF, F32)
        acc_ref[slot] = jnp.zeros(acc_ref.shape[1:], F32)

    def scores(qi, j, h):
        if (qi, h) not in m_cur:
            setup(qi)
        s = jnp.dot(k_ref[j * blk:(j + 1) * blk, h * HEAD:(h + 1) * HEAD], qs_ref[qi % Q_BLOCKS_PER_REGION, h],
                    preferred_element_type=F32)
        if j == qi:
            s = jnp.where(visible, s, NEG_INF)
        m_prev = m_cur[qi, h]
        m_cur[qi, h] = jnp.maximum(m_prev, jnp.max(s, axis=0, keepdims=True))
        return qi, j, h, s, m_prev, m_cur[qi, h]

    def accumulate(qi, j, h, s, m_prev, m_new):
        alpha = jnp.exp2(m_prev - m_new)
        p = jnp.exp2(s - m_new).astype(BF16)
        pv = jnp.dot(vt_ref[j, h * VAUG:(h + 1) * VAUG, :], p, preferred_element_type=F32)
        acc_ref[qi % Q_BLOCKS_PER_REGION, h] = alpha * acc_ref[qi % Q_BLOCKS_PER_REGION, h] + pv

    def finish(qi, h):
        acc = acc_ref[qi % Q_BLOCKS_PER_REGION, h]
        o = acc[:HEAD, :] * (1.0 / acc[HEAD:HEAD + 1, :])
        o = o[:, :blk] - lam * o[:, blk:]
        o = o * lax.rsqrt(jnp.mean(o * o, axis=0, keepdims=True) + SUBLN_EPS) * out_gain
        o_ref[qi * blk:(qi + 1) * blk, h * HEAD:(h + 1) * HEAD] = o.T.astype(BF16)

    pending = [scores(*it) for it in items[:SKEW]]
    for i in range(len(items)):
        current = pending.pop(0)
        if i + SKEW < len(items):
            pending.append(scores(*items[i + SKEW]))
        accumulate(*current)
        qi, j, h = current[:3]
        if j == qi:
            finish(qi, h)


def _diff_attention_prompt(lams, subln_col, dqt, dkb, dvt, batch, seq, lambda_init):
    blk = ATTN_BLK
    nq = seq // blk
    vec = _const_spec((1, DIFF_DK))
    return pl.pallas_call(
        functools.partial(_attn_kernel, lambda_init=lambda_init),
        grid=(batch,),
        in_specs=[vec] * 4 + [_const_spec((HEAD, 1)),
                  pl.BlockSpec((nq, SEG, blk), lambda b: (b, 0, 0)),
                  pl.BlockSpec((seq, SEG), lambda b: (b, 0)),
                  pl.BlockSpec((nq, DIFF_HEADS * VAUG, blk), lambda b: (b, 0, 0))],
        out_specs=pl.BlockSpec((seq, SEG), lambda b: (b, 0)),
        out_shape=jax.ShapeDtypeStruct((batch * seq, SEG), BF16),
        scratch_shapes=[pltpu.VMEM((Q_BLOCKS_PER_REGION, DIFF_HEADS,HEAD, 2 * blk), BF16),
                        pltpu.VMEM((Q_BLOCKS_PER_REGION, DIFF_HEADS,VAUG, 2 * blk), F32)],
        compiler_params=_params("parallel"),
        name="diff_attn_prompt",
    )(*lams, subln_col, dqt, dkb, dvt)


def _attn_sample_kernel(lq1_ref, lk1_ref, lq2_ref, lk2_ref, sub_ref, q_ref, kpt_ref, vp_ref,
                        kn_ref, vn_ref, o_ref, *, lambda_init):
    n = q_ref.shape[0]
    past = kpt_ref.shape[2]
    lam = _lambda(lq1_ref, lk1_ref, lq2_ref, lk2_ref, lambda_init)
    for h in range(DIFF_HEADS):
        sl = slice(h * HEAD, (h + 1) * HEAD)
        qs = _stack_heads(q_ref[:, sl])
        s_p = jnp.dot(qs, kpt_ref[0, sl, :].astype(BF16), preferred_element_type=F32)
        s_n = lax.dot_general(qs, kn_ref[:, sl], _NT, preferred_element_type=F32)
        m = jnp.maximum(jnp.max(s_p, axis=-1, keepdims=True), jnp.max(s_n, axis=-1, keepdims=True))
        p_p = jnp.exp2(s_p - m)
        p_n = jnp.exp2(s_n - m)
        l = jnp.sum(p_p, axis=-1, keepdims=True) + jnp.sum(p_n, axis=-1, keepdims=True)
        v_p = vp_ref[0, pl.ds(h, past, stride=DIFF_HEADS), :].astype(BF16)
        acc = (jnp.dot(p_p.astype(BF16), v_p, preferred_element_type=F32)
               + jnp.dot(p_n.astype(BF16), vn_ref[:, sl], preferred_element_type=F32))
        o_ref[:, sl] = _diff_finish(acc, l, n, lam, sub_ref[...], lambda_init).astype(BF16)


def _diff_attention_sample(lams, subln, dq, kt_past, v_past, dkb, dvb, batch, seq, lambda_init):
    past = kt_past.shape[2]
    vec = _const_spec((1, DIFF_DK))
    new = pl.BlockSpec((seq, SEG), lambda b: (b, 0))
    return pl.pallas_call(
        functools.partial(_attn_sample_kernel, lambda_init=lambda_init),
        grid=(batch,),
        in_specs=[vec] * 4 + [_const_spec((1, HEAD)), new,
                  pl.BlockSpec((1, SEG, past), lambda b: (b, 0, 0)),
                  pl.BlockSpec((1, DIFF_HEADS * past, HEAD), lambda b: (b, 0, 0)), new, new],
        out_specs=new,
        out_shape=jax.ShapeDtypeStruct((batch * seq, SEG), BF16),
        compiler_params=_params("parallel"),
        name="diff_attn_sample",
    )(*lams, subln, dq, kt_past, v_past, dkb, dvb)


def _rms(x, w):
    return x * lax.rsqrt(jnp.mean(x * x, axis=-1, keepdims=True) + NORM_EPS) * w


def _mlp_kernel(ret_ref, dif_ref, x_ref, wo_ref, n1_ref, n2_ref, n3_ref, wu_ref, wd_ref, y_ref, u_ref,
                *, ff_blk):
    tm = x_ref.shape[0]
    n_parts = max(1, min(MLP_ROW_PARTS, tm // MLP_MIN_PART_ROWS))
    part = tm // n_parts
    parts = [slice(i * part, (i + 1) * part) for i in range(n_parts)]

    def front(rows):
        mix = (jnp.dot(ret_ref[rows, :], wo_ref[:SEG, :], preferred_element_type=F32)
               + jnp.dot(dif_ref[rows, :], wo_ref[SEG:, :], preferred_element_type=F32))
        x1 = x_ref[rows, :] + _rms(mix, n1_ref[...])
        return x1, _rms(x1, n2_ref[...]).astype(BF16)

    def up(h, rows, c):
        sl = slice(c * ff_blk, (c + 1) * ff_blk)
        u = jnp.maximum(jnp.dot(h, wu_ref[:, sl], preferred_element_type=F32), 0.0)
        u_ref[rows, sl] = (u * u).astype(BF16)

    fronts = [front(rows) for rows in parts]
    for rows, (_, h_part) in zip(parts, fronts):
        up(h_part, rows, 0)
    h = jnp.concatenate([h_part for _, h_part in fronts], axis=0)
    for c in range(1, D_FF // ff_blk):
        up(h, slice(None), c)
    for rows, (x1, _) in zip(parts, fronts):
        f = jnp.dot(u_ref[rows, :], wd_ref[...], preferred_element_type=F32)
        y_ref[rows, :] = x1 + _rms(f, n3_ref[...])


def _out_mlp(ret_out, diff_out, x, w_out, n1, n2, n3, w_up, w_down, tm, ff_blk=1024):
    t = x.shape[0]
    half = pl.BlockSpec((tm, SEG), lambda i: (i, 0))
    row = pl.BlockSpec((tm, D_MODEL), lambda i: (i, 0))
    nrm = _const_spec((1, D_MODEL))
    return pl.pallas_call(
        functools.partial(_mlp_kernel, ff_blk=ff_blk),
        grid=(t // tm,),
        in_specs=[half, half, row, _const_spec((D_MODEL, D_MODEL)), nrm, nrm, nrm,
                  _const_spec((D_MODEL, D_FF)), _const_spec((D_FF, D_MODEL))],
        out_specs=row,
        out_shape=jax.ShapeDtypeStruct((t, D_MODEL), F32),
        scratch_shapes=[pltpu.VMEM((tm, D_FF), BF16)],
        compiler_params=_params("parallel"),
        name="out_mlp",
    )(ret_out, diff_out, x, w_out, n1, n2, n3, w_up, w_down)


def _layer(x, pos0, state0, k_past, v_past, w, lambda_init, tm_proj, tm, ret_blk):
    batch, seq, _ = x.shape
    prompt = k_past is None
    xf = x.reshape(batch * seq, D_MODEL)
    tables = _rope_tables(max(seq, tm_proj), seq, pos0)
    rq, rk, rv, rg, dq, dk, dv, dkb, dvb = _project(xf, w["norm_mix_pre"], w["w_in"], tables, tm_proj, seq,
                                                    transposed=prompt)
    ret_out, new_state = _retention(rq, rk, rv, rg, state0, batch, seq, ret_blk,
                                    blocks_per_iter=RET_BLOCKS_PER_ITER if prompt else 1)
    lams = (w["lambda_q1"], w["lambda_k1"], w["lambda_q2"], w["lambda_k2"])
    if prompt:
        diff_out = _diff_attention_prompt(lams, w["diff_subln"].reshape(HEAD, 1), dq, dkb, dvb,
                                          batch, seq, lambda_init)
        k_new = dk.reshape(batch, 2 * DIFF_HEADS, DIFF_DK, seq).transpose(0, 3, 1, 2)
    else:
        past = k_past.shape[1]
        kt_past = k_past.transpose(0, 2, 3, 1).reshape(batch, SEG, past)
        diff_out = _diff_attention_sample(
            lams, w["diff_subln"], dq, kt_past, v_past.reshape(batch, past * DIFF_HEADS, HEAD),
            dkb, dvb, batch, seq, lambda_init)
        k_new = dk.reshape(batch, seq, 2 * DIFF_HEADS, DIFF_DK)
    y = _out_mlp(ret_out, diff_out, xf, w["w_out"], w["norm_mix_post"], w["norm_ffn_pre"],
                 w["norm_ffn_post"], w["w_up"], w["w_down"], tm)
    return (y.reshape(batch, seq, D_MODEL), k_new, dv.reshape(batch, seq, DIFF_HEADS, HEAD), new_state)


def kernel(x_prompt, x_sample, cache_diff_k, cache_diff_v, state_ret, norm_mix_pre, norm_mix_post, w_in, diff_subln, lambda_q1, lambda_k1, lambda_q2, lambda_k2, w_out, norm_ffn_pre, norm_ffn_post, w_up, w_down):
    depth = w_in.shape[0]
    past = cache_diff_k.shape[2]
    yp, ys = x_prompt, x_sample
    outs = [[] for _ in range(6)]
    for l in range(depth):
        lambda_init = 0.8 - 0.6 * math.exp(-0.3 * l)
        w = dict(
            norm_mix_pre=norm_mix_pre[l][None], norm_mix_post=norm_mix_post[l][None],
            w_in=w_in[l].astype(BF16), diff_subln=diff_subln[l][None],
            lambda_q1=lambda_q1[l][None], lambda_k1=lambda_k1[l][None],
            lambda_q2=lambda_q2[l][None], lambda_k2=lambda_k2[l][None],
            w_out=w_out[l].astype(BF16), norm_ffn_pre=norm_ffn_pre[l][None],
            norm_ffn_post=norm_ffn_post[l][None],
            w_up=w_up[l].astype(BF16), w_down=w_down[l].astype(BF16))
        yp, kp, vp, sp = _layer(yp, 0, None, None, None, w, lambda_init,
                                tm_proj=TM_PROJ, tm=TM_MLP, ret_blk=RET_BLK)
        n_sample = x_sample.shape[0] * x_sample.shape[1]
        ys, ks, vs, ss = _layer(ys, past, state_ret[l], cache_diff_k[l], cache_diff_v[l], w, lambda_init,
                                tm_proj=n_sample, tm=n_sample, ret_blk=x_sample.shape[1])
        for lst, val in zip(outs, (kp, vp, sp, ks, vs, ss)):
            lst.append(val)
    return (yp, ys) + tuple(jnp.stack(o) for o in outs)
```
